```python
import math, functools
import jax, jax.numpy as jnp
from jax import lax
import numpy as np

D_MODEL = 2048
BATCH = 8
SEQ = 2048
DEPTH = 1
DEC_BATCH = 32
DEC_SEQ = 4
PAST_LEN = 8192
PAGE_SIZE = 128

C_CONV = 1024
CONV_W = 31
N_HEADS = 8
DIFF_HD = 64
K_HD = 2 * DIFF_HD
V_HD = 2 * DIFF_HD
Q_BLOCK = 128
N_EXPERTS = 32
TOP_K = 4
D_FF = 2048
SWIGLU_LIMIT = 7.0
SWIGLU_ALPHA = 1.702
MOE_BLOCK = 128
RMS_EPS = 1e-6
LN_EPS = 1e-5
IN_COLS = 2 * C_CONV + 2 * N_HEADS * K_HD + N_HEADS * V_HD + 2 * D_MODEL

kernel_name = 'gated_conformer_diffattn_moe_step'


def rms_norm(x, g, eps=RMS_EPS):
    xf = x.astype(jnp.float32)
    y = xf * lax.rsqrt(jnp.mean(xf * xf, axis=-1, keepdims=True) + eps)
    return (y * g.astype(jnp.float32)).astype(x.dtype)


def layer_norm(x, g, b, eps=LN_EPS):
    xf = x.astype(jnp.float32)
    mu = jnp.mean(xf, axis=-1, keepdims=True)
    var = jnp.mean(jnp.square(xf - mu), axis=-1, keepdims=True)
    y = (xf - mu) * lax.rsqrt(var + eps) * g.astype(jnp.float32) + b.astype(jnp.float32)
    return y.astype(x.dtype)


def lambda_init_fn(layer):
    return 0.8 - 0.6 * math.exp(-0.3 * layer)


def split_in_proj(h, w_in, q_g, k_g):
    z = jnp.einsum('btd,dc->btc', h, w_in)
    b, t = z.shape[:2]
    o1 = 2 * C_CONV
    o2 = o1 + N_HEADS * K_HD
    o3 = o2 + N_HEADS * K_HD
    o4 = o3 + N_HEADS * V_HD
    o5 = o4 + D_MODEL
    glu_in, q, k, v, g_conv, g_attn = jnp.split(z, [o1, o2, o3, o4, o5], axis=-1)
    q = rms_norm(q.reshape(b, t, N_HEADS, 2, DIFF_HD), q_g)
    k = rms_norm(k.reshape(b, t, N_HEADS, 2, DIFF_HD), k_g)
    v = v.reshape(b, t, N_HEADS, V_HD)
    return glu_in, q, k, v, g_conv, g_attn


def conv_module(glu_in, left, dw_w, dw_b, ln_g, ln_b, w_proj):
    a, gate = jnp.split(glu_in, 2, axis=-1)
    u = a * jax.nn.sigmoid(gate)
    u_ext = jnp.concatenate([left, u], axis=1)
    c = lax.conv_general_dilated(u_ext, dw_w[:, None, :], (1,), 'VALID',
                                 dimension_numbers=('NWC', 'WIO', 'NWC'),
                                 feature_group_count=C_CONV) + dw_b
    c = layer_norm(c, ln_g, ln_b)
    c = c * jax.nn.sigmoid(c)
    out = jnp.einsum('btc,cd->btd', c, w_proj)
    return out, u_ext[:, -(CONV_W - 1):]


def diff_attn_prompt(q, k, v, lam):
    b, s = q.shape[:2]
    nqb = s // Q_BLOCK
    qb = q.reshape(b, nqb, Q_BLOCK, N_HEADS, 2, DIFF_HD).transpose(1, 0, 2, 3, 4, 5)
    k_pos = jnp.arange(s)
    scale = DIFF_HD ** -0.5

    def block(args):
        q_blk, start = args
        sc = jnp.einsum('bqhmd,bkhmd->bhmqk', q_blk, k,
                        preferred_element_type=jnp.float32) * scale
        q_pos = start + jnp.arange(Q_BLOCK)
        sc = jnp.where(k_pos[None, :] <= q_pos[:, None], sc, -jnp.inf)
        p = jax.nn.softmax(sc, axis=-1)
        pd = p[:, :, 0] - lam * p[:, :, 1]
        return jnp.einsum('bhqk,bkhe->bqhe', pd.astype(v.dtype), v)

    o = lax.map(block, (qb, jnp.arange(nqb) * Q_BLOCK))
    return o.transpose(1, 0, 2, 3, 4).reshape(b, s, N_HEADS, V_HD)


def diff_attn_sample(q, k, v, k_past, v_past, lam):
    t = q.shape[1]
    p_len = k_past.shape[1]
    scale = DIFF_HD ** -0.5
    s_past = jnp.einsum('bqhmd,bkhmd->bhmqk', q, k_past,
                        preferred_element_type=jnp.float32) * scale
    s_new = jnp.einsum('bqhmd,bkhmd->bhmqk', q, k,
                       preferred_element_type=jnp.float32) * scale
    s_new = jnp.where(jnp.tril(jnp.ones((t, t), bool)), s_new, -jnp.inf)
    p = jax.nn.softmax(jnp.concatenate([s_past, s_new], axis=-1), axis=-1)
    pd = (p[:, :, 0] - lam * p[:, :, 1]).astype(v.dtype)
    return (jnp.einsum('bhqk,bkhe->bqhe', pd[..., :p_len], v_past)
            + jnp.einsum('bhqk,bkhe->bqhe', pd[..., p_len:], v))


def moe_ffn(h, router_w, router_b, w_gate_up, b_gate_up, w_down, b_down):
    b, t, d = h.shape
    xt = h.reshape(-1, d)
    n = xt.shape[0]
    nk = n * TOP_K
    logits = (xt @ router_w).astype(jnp.float32) + router_b.astype(jnp.float32)
    top_val, top_idx = lax.top_k(logits, TOP_K)
    gate_w = jax.nn.softmax(top_val, axis=-1)
    flat_e = top_idx.reshape(-1)
    flat_tok = jnp.repeat(jnp.arange(n, dtype=jnp.int32), TOP_K)
    flat_w = gate_w.reshape(-1)
    order = jnp.argsort(flat_e)
    sorted_e = flat_e[order]
    counts = jnp.bincount(flat_e, length=N_EXPERTS)
    starts = jnp.cumsum(counts) - counts
    padded = (counts + MOE_BLOCK - 1) // MOE_BLOCK * MOE_BLOCK
    pad_ends = jnp.cumsum(padded)
    pad_starts = pad_ends - padded
    dest = pad_starts[sorted_e] + jnp.arange(nk) - starts[sorted_e]
    n_blocks = -(-(nk + N_EXPERTS * (MOE_BLOCK - 1)) // MOE_BLOCK)
    n_rows = n_blocks * MOE_BLOCK
    row_tok = jnp.zeros((n_rows,), jnp.int32).at[dest].set(flat_tok[order])
    row_w = jnp.zeros((n_rows,), jnp.float32).at[dest].set(flat_w[order])
    block_e = jnp.minimum(jnp.searchsorted(pad_ends, jnp.arange(n_blocks) * MOE_BLOCK, side='right'),
                          N_EXPERTS - 1)
    xs = xt[row_tok].reshape(n_blocks, MOE_BLOCK, d)

    def expert_block(args):
        xb, e = args
        gu = xb @ w_gate_up[e] + b_gate_up[e]
        g, u = jnp.split(gu, 2, axis=-1)
        g = jnp.minimum(g, SWIGLU_LIMIT)
        u = jnp.clip(u, -SWIGLU_LIMIT, SWIGLU_LIMIT)
        a = (u + 1) * (g * jax.nn.sigmoid(SWIGLU_ALPHA * g))
        return a @ w_down[e] + b_down[e]

    ys = lax.map(expert_block, (xs, block_e)).reshape(n_rows, d)
    y = jnp.zeros((n, d), jnp.float32).at[row_tok].add(ys.astype(jnp.float32) * row_w[:, None])
    return y.astype(h.dtype).reshape(b, t, d)


def decoder_layer(x, conv_left, attn_core, lam_init, lw):
    b, t = x.shape[:2]
    h = rms_norm(x, lw['norm1_g'])
    glu_in, q, k, v, g_conv, g_attn = split_in_proj(h, lw['w_in'], lw['q_norm_g'], lw['k_norm_g'])
    c_out, conv_new = conv_module(glu_in, conv_left, lw['conv_dw_w'], lw['conv_dw_b'],
                                  lw['conv_ln_g'], lw['conv_ln_b'], lw['w_conv_out'])
    o = attn_core(q, k, v)
    o = rms_norm(o, lw['subln_g']) * (1.0 - lam_init)
    a_out = jnp.einsum('btf,fd->btd', o.reshape(b, t, N_HEADS * V_HD), lw['w_attn_out'])
    merged = jax.nn.sigmoid(g_conv) * c_out + jax.nn.sigmoid(g_attn) * a_out
    x = x + jnp.einsum('btd,de->bte', merged, lw['w_out'])
    x = x + moe_ffn(rms_norm(x, lw['norm2_g']), lw['router_w'], lw['router_b'],
                    lw['w_gate_up'], lw['b_gate_up'], lw['w_down'], lw['b_down'])
    return x, k.reshape(b, t, N_HEADS, K_HD), v, conv_new


def setup_inputs(seed: int = 0) -> dict:
    key = jax.random.key(seed)
    ks = jax.random.split(key, 32)
    f32 = jnp.float32
    n_pages = PAST_LEN // PAGE_SIZE
    n_used = DEC_BATCH * n_pages
    n_phys = n_used + max(1, n_used // 4)

    def nrm(k, shape, scale):
        return jax.random.normal(k, shape, f32) * scale

    def gain(k, shape):
        return 1.0 + nrm(k, shape, 0.02)

    page_table = jax.random.permutation(ks[5], n_phys)[:n_used].reshape(DEC_BATCH, n_pages).astype(jnp.int32)
    return {
        'x_prompt': nrm(ks[0], (BATCH, SEQ, D_MODEL), 1.0),
        'x_sample': nrm(ks[1], (DEC_BATCH, DEC_SEQ, D_MODEL), 1.0),
        'cache_k': nrm(ks[2], (DEPTH, n_phys, PAGE_SIZE, N_HEADS, K_HD), 1.0),
        'cache_v': nrm(ks[3], (DEPTH, n_phys, PAGE_SIZE, N_HEADS, V_HD), 1.0),
        'state_conv': nrm(ks[4], (DEPTH, DEC_BATCH, CONV_W - 1, C_CONV), 0.5),
        'page_table': page_table,
        'norm1_g': gain(ks[6], (DEPTH, D_MODEL)),
        'w_in': nrm(ks[7], (DEPTH, D_MODEL, IN_COLS), D_MODEL ** -0.5),
        'conv_dw_w': nrm(ks[8], (DEPTH, CONV_W, C_CONV), CONV_W ** -0.5),
        'conv_dw_b': nrm(ks[9], (DEPTH, C_CONV), 0.02),
        'conv_ln_g': gain(ks[10], (DEPTH, C_CONV)),
        'conv_ln_b': nrm(ks[11], (DEPTH, C_CONV), 0.02),
        'w_conv_out': nrm(ks[12], (DEPTH, C_CONV, D_MODEL), C_CONV ** -0.5),
        'q_norm_g': gain(ks[13], (DEPTH, DIFF_HD)),
        'k_norm_g': gain(ks[14], (DEPTH, DIFF_HD)),
        'lambda_q1': nrm(ks[15], (DEPTH, DIFF_HD), 0.1),
        'lambda_k1': nrm(ks[16], (DEPTH, DIFF_HD), 0.1),
        'lambda_q2': nrm(ks[17], (DEPTH, DIFF_HD), 0.1),
        'lambda_k2': nrm(ks[18], (DEPTH, DIFF_HD), 0.1),
        'subln_g': gain(ks[19], (DEPTH, V_HD)),
        'w_attn_out': nrm(ks[20], (DEPTH, N_HEADS * V_HD, D_MODEL), (N_HEADS * V_HD) ** -0.5),
        'w_out': nrm(ks[21], (DEPTH, D_MODEL, D_MODEL), D_MODEL ** -0.5),
        'norm2_g': gain(ks[22], (DEPTH, D_MODEL)),
        'router_w': nrm(ks[23], (DEPTH, D_MODEL, N_EXPERTS), D_MODEL ** -0.5),
        'router_b': nrm(ks[24], (DEPTH, N_EXPERTS), 0.01),
        'w_gate_up': nrm(ks[25], (DEPTH, N_EXPERTS, D_MODEL, 2 * D_FF), D_MODEL ** -0.5),
        'b_gate_up': nrm(ks[26], (DEPTH, N_EXPERTS, 2 * D_FF), 0.02),
        'w_down': nrm(ks[27], (DEPTH, N_EXPERTS, D_FF, D_MODEL), D_FF ** -0.5),
        'b_down': nrm(ks[28], (DEPTH, N_EXPERTS, D_MODEL), 0.02),
    }


def reference(x_prompt, x_sample, cache_k, cache_v, state_conv, page_table,
              norm1_g, w_in, conv_dw_w, conv_dw_b, conv_ln_g, conv_ln_b, w_conv_out,
              q_norm_g, k_norm_g, lambda_q1, lambda_k1, lambda_q2, lambda_k2, subln_g,
              w_attn_out, w_out, norm2_g, router_w, router_b,
              w_gate_up, b_gate_up, w_down, b_down):
    f32 = jnp.float32
    db = x_sample.shape[0]
    yp, ys = x_prompt, x_sample
    kp_l, vp_l, cp_l, ks_l, vs_l, cs_l = [], [], [], [], [], []
    for l in range(DEPTH):
        lam_init = lambda_init_fn(l)
        lam = (jnp.exp(jnp.sum(lambda_q1[l].astype(f32) * lambda_k1[l].astype(f32)))
               - jnp.exp(jnp.sum(lambda_q2[l].astype(f32) * lambda_k2[l].astype(f32))) + lam_init)
        lw = dict(norm1_g=norm1_g[l], w_in=w_in[l], conv_dw_w=conv_dw_w[l], conv_dw_b=conv_dw_b[l],
                  conv_ln_g=conv_ln_g[l], conv_ln_b=conv_ln_b[l], w_conv_out=w_conv_out[l],
                  q_norm_g=q_norm_g[l], k_norm_g=k_norm_g[l], subln_g=subln_g[l],
                  w_attn_out=w_attn_out[l], w_out=w_out[l], norm2_g=norm2_g[l],
                  router_w=router_w[l], router_b=router_b[l], w_gate_up=w_gate_up[l],
                  b_gate_up=b_gate_up[l], w_down=w_down[l], b_down=b_down[l])
        conv_left_p = jnp.zeros((yp.shape[0], CONV_W - 1, C_CONV), yp.dtype)
        yp, kp, vp, cp = decoder_layer(yp, conv_left_p,
                                       functools.partial(diff_attn_prompt, lam=lam), lam_init, lw)
        k_past = cache_k[l][page_table].reshape(db, -1, N_HEADS, 2, DIFF_HD)
        v_past = cache_v[l][page_table].reshape(db, -1, N_HEADS, V_HD)
        ys, kn, vn, cn = decoder_layer(ys, state_conv[l],
                                       functools.partial(diff_attn_sample, k_past=k_past,
                                                         v_past=v_past, lam=lam), lam_init, lw)
        kp_l.append(kp); vp_l.append(vp); cp_l.append(cp)
        ks_l.append(kn); vs_l.append(vn); cs_l.append(cn)
    return (yp, ys, jnp.stack(kp_l), jnp.stack(vp_l), jnp.stack(cp_l),
            jnp.stack(ks_l), jnp.stack(vs_l), jnp.stack(cs_l))
```

```python
import functools
import math

import jax
import jax.numpy as jnp
from jax import lax
from jax.experimental import pallas as pl
from jax.experimental.pallas import tpu as pltpu

F32, BF16, I32 = jnp.float32, jnp.bfloat16, jnp.int32

D_MODEL = 2048
C_CONV = 1024
CONV_W = 31
N_HEADS = 8
DIFF_HD = 64
HEAD_W = 2 * DIFF_HD
QKV_W = N_HEADS * HEAD_W
N_EXPERTS = 32
TOP_K = 4
D_FF = 2048
SWIGLU_LIMIT = 7.0
SWIGLU_ALPHA = 1.702
RMS_EPS = 1e-6
LN_EPS = 1e-5
LAM_INIT = 0.8 - 0.6 * math.exp(-0.3 * 0)
PAGE = 128
MASK_VALUE = -1e30

LANES = 128
MXU_DIM = 256

ROW_TILE = 384
CONV_TILE = 256
CONV_CHUNK = 32
CONV_HALO = 32
ATTN_TILE = 512
DEC_PAGES = 4
MERGE_TILE = 256
MOE_SUB = 256
MOE_CAP = 2304
MOE_FF = 256
DISPATCH_TILE = 384
COMBINE_TILE = 256


def _cparams(semantics, vmem_mb=None):
    kw = dict(dimension_semantics=semantics)
    if vmem_mb is not None:
        kw["vmem_limit_bytes"] = vmem_mb * 1024 * 1024
    return pltpu.CompilerParams(**kw)


def _sigmoid(x):
    return 1.0 / (1.0 + jnp.exp(-x))


def _dot(a, b):
    return jnp.dot(a, b, preferred_element_type=F32)


def _rms(x, g):
    ms = jnp.mean(x * x, axis=-1, keepdims=True)
    return x * lax.rsqrt(ms + RMS_EPS) * g


def _rms_body(x_ref, g_ref, *rest):
    o_ref = rest[-1]
    o_ref[...] = _rms(x_ref[...], g_ref[...]).astype(o_ref.dtype)


def _rms_into(x, g, total_rows, tile, block0, prev=None):
    n = x.shape[0]
    in_specs = [pl.BlockSpec((tile, D_MODEL), lambda i: (i, 0)),
                pl.BlockSpec((1, D_MODEL), lambda i: (0, 0))]
    args = [x, g]
    aliases = {}
    if prev is not None:
        in_specs.append(pl.BlockSpec(memory_space=pl.ANY))
        args.append(prev)
        aliases = {2: 0}
    return pl.pallas_call(
        _rms_body,
        out_shape=jax.ShapeDtypeStruct((total_rows, D_MODEL), BF16),
        grid=(n // tile,),
        in_specs=in_specs,
        out_specs=pl.BlockSpec((tile, D_MODEL), lambda i: (block0 + i, 0)),
        input_output_aliases=aliases,
        compiler_params=_cparams(("arbitrary",)),
        name="rms_in",
    )(*args)


def _glu_body(h_ref, wa_ref, wg_ref, u_ref):
    h = h_ref[...]
    a = _dot(h, wa_ref[...])
    g = _dot(h, wg_ref[...])
    u_ref[...] = a * _sigmoid(g)


def _glu_proj(h, w_in):
    n = h.shape[0]
    return pl.pallas_call(
        _glu_body,
        out_shape=jax.ShapeDtypeStruct((n, C_CONV), F32),
        grid=(n // ROW_TILE,),
        in_specs=[pl.BlockSpec((ROW_TILE, D_MODEL), lambda i: (i, 0)),
                  pl.BlockSpec((D_MODEL, C_CONV), lambda i: (0, 0)),
                  pl.BlockSpec((D_MODEL, C_CONV), lambda i: (0, 1))],
        out_specs=pl.BlockSpec((ROW_TILE, C_CONV), lambda i: (i, 0)),
        compiler_params=_cparams(("arbitrary",), 48),
        name="glu_proj",
    )(h, w_in, w_in)


def _group_rms(z, g, gmat):
    zz = z * z
    hi = zz.astype(BF16)
    lo = (zz - hi.astype(F32)).astype(BF16)
    cols = []
    for c in range(QKV_W // MXU_DIM):
        sl = slice(c * MXU_DIM, (c + 1) * MXU_DIM)
        cols.append(_dot(hi[:, sl], gmat) + _dot(lo[:, sl], gmat))
    ss = jnp.concatenate(cols, axis=1)
    return z * lax.rsqrt(ss * (1.0 / DIFF_HD) + RMS_EPS) * g


def _qkv_body(h_ref, wq_ref, wk_ref, wv_ref, gq_ref, gk_ref, gm_ref,
              q_ref, kf_ref, kb_ref, vf_ref, vb_ref):
    h = h_ref[...]
    gmat = gm_ref[...]
    q = _group_rms(_dot(h, wq_ref[...]), gq_ref[...], gmat)
    q_ref[...] = (q * (DIFF_HD ** -0.5)).astype(BF16)
    k = _group_rms(_dot(h, wk_ref[...]), gk_ref[...], gmat)
    kf_ref[...] = k
    kb_ref[...] = k.astype(BF16)
    v = _dot(h, wv_ref[...])
    vf_ref[...] = v
    vb_ref[...] = v.astype(BF16)


def _qkv_proj(h, w_in, gq, gk, gmat):
    n = h.shape[0]
    wspec = lambda j: pl.BlockSpec((D_MODEL, QKV_W), lambda i: (0, j))
    row = lambda w: pl.BlockSpec((ROW_TILE, w), lambda i: (i, 0))
    const = lambda r, c: pl.BlockSpec((r, c), lambda i: (0, 0))
    sds = lambda dt: jax.ShapeDtypeStruct((n, QKV_W), dt)
    return pl.pallas_call(
        _qkv_body,
        out_shape=(sds(BF16), sds(F32), sds(BF16), sds(F32), sds(BF16)),
        grid=(n // ROW_TILE,),
        in_specs=[row(D_MODEL), wspec(2), wspec(3), wspec(4),
                  const(1, QKV_W), const(1, QKV_W), const(MXU_DIM, MXU_DIM)],
        out_specs=(row(QKV_W),) * 5,
        compiler_params=_cparams(("arbitrary",), 56),
        name="qkv_proj",
    )(h, w_in, w_in, w_in, gq, gk, gmat)


def _gate_body(h_ref, w_ref, o_ref):
    o_ref[...] = _sigmoid(_dot(h_ref[...], w_ref[...]))


def _gate_proj(h, w_in):
    n = h.shape[0]
    first = (2 * C_CONV + 3 * QKV_W) // QKV_W
    return pl.pallas_call(
        _gate_body,
        out_shape=jax.ShapeDtypeStruct((n, 2 * D_MODEL), F32),
        grid=(2 * D_MODEL // QKV_W, n // ROW_TILE),
        in_specs=[pl.BlockSpec((ROW_TILE, D_MODEL), lambda j, i: (i, 0)),
                  pl.BlockSpec((D_MODEL, QKV_W), lambda j, i: (0, first + j))],
        out_specs=pl.BlockSpec((ROW_TILE, QKV_W), lambda j, i: (i, j)),
        compiler_params=_cparams(("arbitrary", "arbitrary"), 40),
        name="gate_proj",
    )(h, w_in)


def _ln_swish(c, g, b):
    mu = jnp.mean(c, axis=-1, keepdims=True)
    d = c - mu
    var = jnp.mean(d * d, axis=-1, keepdims=True)
    y = d * lax.rsqrt(var + LN_EPS) * g + b
    return y * _sigmoid(y)


def _conv_body(u_ref, w_ref, b_ref, g_ref, beta_ref, o_ref, ext_ref):
    i = pl.program_id(1)

    @pl.when(i == 0)
    def _():
        ext_ref[0:CONV_HALO, :] = jnp.zeros((CONV_HALO, C_CONV), F32)

    @pl.when(i > 0)
    def _():
        ext_ref[0:CONV_HALO, :] = ext_ref[CONV_TILE:CONV_TILE + CONV_HALO, :]

    ext_ref[CONV_HALO:, :] = u_ref[...]
    base = CONV_HALO - (CONV_W - 1)
    for r in range(CONV_TILE // CONV_CHUNK):
        r0 = r * CONV_CHUNK
        acc = jnp.zeros((CONV_CHUNK, C_CONV), F32) + b_ref[...]
        for j in range(CONV_W):
            acc = acc + ext_ref[r0 + base + j:r0 + base + j + CONV_CHUNK, :] * w_ref[j:j + 1, :]
        y = _ln_swish(acc, g_ref[...], beta_ref[...])
        o_ref[r0:r0 + CONV_CHUNK, :] = y.astype(o_ref.dtype)


def _conv_prompt(u_all, n_batch, seq, dw_w, dw_b, ln_g, ln_b):
    nt = seq // CONV_TILE
    const = lambda r: pl.BlockSpec((r, C_CONV), lambda b, i: (0, 0))
    return pl.pallas_call(
        _conv_body,
        out_shape=jax.ShapeDtypeStruct((n_batch * seq, C_CONV), BF16),
        grid=(n_batch, nt),
        in_specs=[pl.BlockSpec((CONV_TILE, C_CONV), lambda b, i: (b * nt + i, 0)),
                  const(CONV_HALO), const(1), const(1), const(1)],
        out_specs=pl.BlockSpec((CONV_TILE, C_CONV), lambda b, i: (b * nt + i, 0)),
        scratch_shapes=[pltpu.VMEM((CONV_HALO + CONV_TILE, C_CONV), F32)],
        compiler_params=_cparams(("arbitrary", "arbitrary")),
        name="conv_prompt",
    )(u_all, dw_w, dw_b, ln_g, ln_b)


def _conv_sample_body(ext_ref, w_ref, b_ref, g_ref, beta_ref, o_ref, *, t_new):
    acc = jnp.zeros((t_new, C_CONV), F32) + b_ref[...]
    for j in range(CONV_W):
        acc = acc + ext_ref[0, j:j + t_new, :] * w_ref[j:j + 1, :]
    y = _ln_swish(acc, g_ref[...], beta_ref[...])
    o_ref[0] = y.astype(o_ref.dtype)


def _conv_sample(ext, dw_w, dw_b, ln_g, ln_b):
    nb, rows, _ = ext.shape
    t_new = rows - (CONV_W - 1)
    const = lambda r: pl.BlockSpec((r, C_CONV), lambda b: (0, 0))
    return pl.pallas_call(
        functools.partial(_conv_sample_body, t_new=t_new),
        out_shape=jax.ShapeDtypeStruct((nb, t_new, C_CONV), BF16),
        grid=(nb,),
        in_specs=[pl.BlockSpec((1, rows, C_CONV), lambda b: (b, 0, 0)),
                  const(CONV_HALO), const(1), const(1), const(1)],
        out_specs=pl.BlockSpec((1, t_new, C_CONV), lambda b: (b, 0, 0)),
        compiler_params=_cparams(("arbitrary",)),
        name="conv_sample",
    )(ext, dw_w, dw_b, ln_g, ln_b)


def _lambda_value(lam_ref):
    lp = lam_ref[...]
    a = jnp.sum(lp[0:1, :] * lp[1:2, :], axis=-1, keepdims=True)
    b = jnp.sum(lp[2:3, :] * lp[3:4, :], axis=-1, keepdims=True)
    return jnp.exp(a) - jnp.exp(b) + LAM_INIT


def _subln(o, g):
    return _rms(o, g) * (1.0 - LAM_INIT)


def _attn_body(q_ref, k_ref, v_ref, lam_ref, g_ref, o_ref, m_ref, l_ref, acc_ref):
    qi = pl.program_id(2)
    ki = pl.program_id(3)
    t = ATTN_TILE

    @pl.when(ki == 0)
    def _():
        m_ref[...] = jnp.full(m_ref.shape, MASK_VALUE, F32)
        l_ref[...] = jnp.zeros(l_ref.shape, F32)
        acc_ref[...] = jnp.zeros(acc_ref.shape, F32)

    def step(diagonal):
        q = q_ref[...]
        k = k_ref[...]
        v = v_ref[...]
        lane = lax.broadcasted_iota(I32, (t, HEAD_W), 1)
        zero = jnp.zeros_like(q)
        halves = (jnp.where(lane < DIFF_HD, q, zero), jnp.where(lane >= DIFF_HD, q, zero))
        if diagonal:
            row = lax.broadcasted_iota(I32, (t, t), 0)
            col = lax.broadcasted_iota(I32, (t, t), 1)
            keep = col <= row
        for m in range(2):
            s = lax.dot_general(halves[m], k, (((1,), (1,)), ((), ())),
                                preferred_element_type=F32)
            if diagonal:
                s = jnp.where(keep, s, MASK_VALUE)
            m_prev = m_ref[m]
            m_new = jnp.maximum(m_prev, jnp.max(s, axis=1, keepdims=True))
            alpha = jnp.exp(m_prev - m_new)
            p = jnp.exp(s - m_new)
            l_ref[m] = alpha * l_ref[m] + jnp.sum(p, axis=1, keepdims=True)
            acc_ref[m] = alpha * acc_ref[m] + _dot(p.astype(BF16), v)
            m_ref[m] = m_new

    @pl.when(ki < qi)
    def _():
        step(False)

    @pl.when(ki == qi)
    def _():
        step(True)
        lam = _lambda_value(lam_ref)
        o = acc_ref[0] / l_ref[0] - lam * (acc_ref[1] / l_ref[1])
        o_ref[...] = _subln(o, g_ref[...]).astype(o_ref.dtype)


def _attn_prompt(q_all, k_all, v_all, n_batch, seq, lam_p, subln_g):
    nq = seq // ATTN_TILE
    qspec = pl.BlockSpec((ATTN_TILE, HEAD_W), lambda b, h, qi, ki: (b * nq + qi, h))
    kspec = pl.BlockSpec((ATTN_TILE, HEAD_W),
                         lambda b, h, qi, ki: (b * nq + jnp.minimum(ki, qi), h))
    return pl.pallas_call(
        _attn_body,
        out_shape=jax.ShapeDtypeStruct((n_batch * seq, QKV_W), BF16),
        grid=(n_batch, N_HEADS, nq, nq),
        in_specs=[qspec, kspec, kspec,
                  pl.BlockSpec((4, DIFF_HD), lambda b, h, qi, ki: (0, 0)),
                  pl.BlockSpec((1, HEAD_W), lambda b, h, qi, ki: (0, 0))],
        out_specs=qspec,
        scratch_shapes=[pltpu.VMEM((2, ATTN_TILE, 1), F32),
                        pltpu.VMEM((2, ATTN_TILE, 1), F32),
                        pltpu.VMEM((2, ATTN_TILE, HEAD_W), F32)],
        compiler_params=_cparams(("arbitrary",) * 4),
        name="attn_prompt",
    )(q_all, k_all, v_all, lam_p, subln_g)


def _dec_body(pt_ref, q_ref, *refs, t_new):
    del pt_ref
    k_refs = refs[:DEC_PAGES]
    v_refs = refs[DEC_PAGES:2 * DEC_PAGES]
    knew_ref, vnew_ref, lam_ref, g_ref, o_ref, m_ref, l_ref, acc_ref = refs[2 * DEC_PAGES:]
    p = pl.program_id(1)
    page_rows = PAGE * N_HEADS
    n_cols = 2 * N_HEADS * t_new

    @pl.when(p == 0)
    def _():
        m_ref[...] = jnp.full(m_ref.shape, MASK_VALUE, F32)
        l_ref[...] = jnp.zeros(l_ref.shape, F32)
        acc_ref[...] = jnp.zeros(acc_ref.shape, F32)

    qcols = q_ref[0]

    def head_mask(rows):
        r = lax.broadcasted_iota(I32, (rows, LANES), 0)
        c = lax.broadcasted_iota(I32, (rows, LANES), 1)
        same_head = (r % N_HEADS) == ((c % (N_HEADS * t_new)) // t_new)
        return r, c, same_head & (c < n_cols)

    def update(kb, vb, keep):
        s = jnp.where(keep, _dot(kb, qcols), MASK_VALUE)
        m_prev = m_ref[...]
        m_new = jnp.maximum(m_prev, jnp.max(s, axis=0, keepdims=True))
        alpha = jnp.exp(m_prev - m_new)
        pr = jnp.exp(s - m_new)
        l_ref[...] = alpha * l_ref[...] + jnp.sum(pr, axis=0, keepdims=True)
        pv = lax.dot_general(vb, pr.astype(BF16), (((0,), (0,)), ((), ())),
                             preferred_element_type=F32)
        acc_ref[...] = alpha * acc_ref[...] + pv
        m_ref[...] = m_new

    _, _, keep_page = head_mask(page_rows)
    for j in range(DEC_PAGES):
        update(k_refs[j][0].astype(BF16), v_refs[j][0].astype(BF16), keep_page)

    @pl.when(p == pl.num_programs(1) - 1)
    def _():
        r, c, keep = head_mask(t_new * N_HEADS)
        causal = (r // N_HEADS) <= (c % t_new)
        update(knew_ref[0], vnew_ref[0], keep & causal)
        o_cols = (acc_ref[...] / l_ref[...]).T
        half = N_HEADS * t_new
        lam = _lambda_value(lam_ref)
        o = o_cols[0:half, :] - lam * o_cols[half:2 * half, :]
        o_ref[0] = _subln(o, g_ref[...]).astype(o_ref.dtype)


def _attn_sample(qcols, cache_k, cache_v, page_table, k_new, v_new, lam_p, subln_g, t_new):
    nb = qcols.shape[0]
    n_pages = page_table.shape[1]
    steps = n_pages // DEC_PAGES
    page_rows = PAGE * N_HEADS
    ck = cache_k.reshape(cache_k.shape[0], page_rows, HEAD_W)
    cv = cache_v.reshape(cache_v.shape[0], page_rows, HEAD_W)
    pt = page_table.reshape(-1)

    def page_spec(j):
        return pl.BlockSpec((1, page_rows, HEAD_W),
                            lambda b, p, pt_ref: (pt_ref[b * n_pages + p * DEC_PAGES + j], 0, 0))

    per_b = lambda r, c: pl.BlockSpec((1, r, c), lambda b, p, pt_ref: (b, 0, 0))
    const = lambda r, c: pl.BlockSpec((r, c), lambda b, p, pt_ref: (0, 0))
    rows_new = t_new * N_HEADS
    grid_spec = pltpu.PrefetchScalarGridSpec(
        num_scalar_prefetch=1,
        grid=(nb, steps),
        in_specs=[per_b(HEAD_W, LANES)]
                 + [page_spec(j) for j in range(DEC_PAGES)] * 2
                 + [per_b(rows_new, HEAD_W), per_b(rows_new, HEAD_W),
                    const(4, DIFF_HD), const(1, HEAD_W)],
        out_specs=per_b(rows_new, HEAD_W),
        scratch_shapes=[pltpu.VMEM((1, LANES), F32), pltpu.VMEM((1, LANES), F32),
                        pltpu.VMEM((HEAD_W, LANES), F32)],
    )
    return pl.pallas_call(
        functools.partial(_dec_body, t_new=t_new),
        out_shape=jax.ShapeDtypeStruct((nb, rows_new, HEAD_W), BF16),
        grid_spec=grid_spec,
        compiler_params=_cparams(("arbitrary", "arbitrary")),
        name="attn_sample",
    )(pt, qcols, *([ck] * DEC_PAGES), *([cv] * DEC_PAGES), k_new, v_new, lam_p, subln_g)


def _merge_body(x_ref, c_ref, o_ref, sc_ref, sa_ref, wc_ref, wa_ref, wo_ref, g2_ref,
                rwh_ref, rwl_ref, rb_ref, *rest):
    x1_ref, idx_ref, gw_ref = rest[-3:]
    tm = x_ref.shape[0]
    c_out = _dot(c_ref[...], wc_ref[...])
    a_out = _dot(o_ref[...], wa_ref[...])
    merged = sc_ref[...] * c_out + sa_ref[...] * a_out
    x1 = x_ref[...] + _dot(merged.astype(BF16), wo_ref[...])
    x1_ref[...] = x1

    h2 = _rms(x1, g2_ref[...])
    hi = h2.astype(BF16)
    lo = (h2 - hi.astype(F32)).astype(BF16)
    logits = (_dot(hi, rwh_ref[...]) + _dot(lo, rwh_ref[...]) + _dot(hi, rwl_ref[...])
              + rb_ref[...])
    lane = lax.broadcasted_iota(I32, (tm, LANES), 1)
    v = jnp.where(lane < N_EXPERTS, logits, -jnp.inf)
    kcol = lax.broadcasted_iota(I32, (tm, TOP_K), 1)
    idx_out = jnp.zeros((tm, TOP_K), I32)
    val_out = jnp.zeros((tm, TOP_K), F32)
    for k in range(TOP_K):
        mx = jnp.max(v, axis=1, keepdims=True)
        ix = jnp.min(jnp.where(v == mx, lane, LANES), axis=1, keepdims=True)
        idx_out = jnp.where(kcol == k, ix, idx_out)
        val_out = jnp.where(kcol == k, mx, val_out)
        v = jnp.where(lane == ix, -jnp.inf, v)
    e = jnp.exp(val_out - val_out[:, 0:1])
    idx_ref[...] = idx_out
    gw_ref[...] = e / jnp.sum(e, axis=1, keepdims=True)


def _merge(x, c_act, o_act, sig_all, wc, wa, wo, g2, rwh, rwl, rb, total_rows, tile, block0,
           prev=None):
    n = x.shape[0]
    row = lambda w: pl.BlockSpec((tile, w), lambda i: (i, 0))
    const = lambda r, c: pl.BlockSpec((r, c), lambda i: (0, 0))
    in_specs = [row(D_MODEL), row(C_CONV), row(QKV_W),
                pl.BlockSpec((tile, D_MODEL), lambda i: (block0 + i, 0)),
                pl.BlockSpec((tile, D_MODEL), lambda i: (block0 + i, 1)),
                const(C_CONV, D_MODEL), const(QKV_W, D_MODEL), const(D_MODEL, D_MODEL),
                const(1, D_MODEL), const(D_MODEL, LANES), const(D_MODEL, LANES),
                const(1, LANES)]
    args = [x, c_act, o_act, sig_all, sig_all, wc, wa, wo, g2, rwh, rwl, rb]
    aliases = {}
    if prev is not None:
        in_specs.append(pl.BlockSpec(memory_space=pl.ANY))
        args.append(prev)
        aliases = {len(args) - 1: 0}
    return pl.pallas_call(
        _merge_body,
        out_shape=(jax.ShapeDtypeStruct((total_rows, D_MODEL), F32),
                   jax.ShapeDtypeStruct((n, TOP_K), I32),
                   jax.ShapeDtypeStruct((n, TOP_K), F32)),
        grid=(n // tile,),
        in_specs=in_specs,
        out_specs=(pl.BlockSpec((tile, D_MODEL), lambda i: (block0 + i, 0)),
                   row(TOP_K), row(TOP_K)),
        input_output_aliases=aliases,
        compiler_params=_cparams(("arbitrary",), 60),
        name="merge_router",
    )(*args)


def _routing_tables(top_idx):
    n = top_idx.shape[0]
    nk = n * TOP_K
    eids = jnp.arange(N_EXPERTS, dtype=I32)
    onehot = (top_idx[:, :, None] == eids[None, None, :]).astype(I32)
    per_tok = jnp.sum(onehot, axis=1)
    csum = jnp.cumsum(per_tok, axis=0)
    counts = csum[-1]
    rank = jnp.take_along_axis(csum - per_tok, top_idx, axis=1)
    padded = (counts + MOE_SUB - 1) // MOE_SUB * MOE_SUB
    gstart = jnp.cumsum(padded) - padded
    pos = (gstart[top_idx] + rank).astype(I32)

    n_sb_max = nk // MOE_CAP + N_EXPERTS
    nsb = (counts + MOE_CAP - 1) // MOE_CAP
    cum = jnp.cumsum(nsb)
    total = cum[-1]
    s = jnp.arange(n_sb_max, dtype=I32)
    ex = jnp.minimum(jnp.searchsorted(cum, s, side="right"), N_EXPERTS - 1).astype(I32)
    ex_last = ex[jnp.maximum(total - 1, 0)]
    valid = s < total
    ex = jnp.where(valid, ex, ex_last)
    local = s - (cum[ex] - nsb[ex])
    rows = jnp.where(valid, jnp.clip(counts[ex] - local * MOE_CAP, 0, MOE_CAP), 0).astype(I32)
    start = jnp.where(valid, gstart[ex] + local * MOE_CAP, 0).astype(I32)
    n_rows = (nk + N_EXPERTS * (MOE_SUB - 1) + MOE_SUB - 1) // MOE_SUB * MOE_SUB
    return pos, ex, start, rows, n_rows


def _dispatch_body(pos_ref, x_ref, g_ref, xs_ref, hbuf, sem):
    i = pl.program_id(0)
    n = pl.num_programs(0)
    tb = x_ref.shape[0]
    slot = lax.rem(i, 2)

    def wait_slot(sl):
        pltpu.make_async_copy(xs_ref.at[pl.ds(0, TOP_K * tb)], xs_ref.at[pl.ds(0, TOP_K * tb)],
                              sem.at[sl]).wait()

    @pl.when(i >= 2)
    def _():
        wait_slot(slot)

    hbuf[slot] = _rms(x_ref[...], g_ref[...])

    def body(t, carry):
        for k in range(TOP_K):
            p = pos_ref[0, 0, t * TOP_K + k]
            pltpu.make_async_copy(hbuf.at[slot, pl.ds(t, 1)], xs_ref.at[pl.ds(p, 1)],
                                  sem.at[slot]).start()
        return carry

    lax.fori_loop(0, tb, body, 0)

    @pl.when(i == n - 1)
    def _():
        wait_slot(slot)
        wait_slot(1 - slot)


def _dispatch(x1_all, g2, pos, n_rows):
    n = x1_all.shape[0]
    tb = DISPATCH_TILE
    steps = n // tb
    assert steps >= 2
    pos3 = pos.reshape(steps, 1, tb * TOP_K)
    return pl.pallas_call(
        _dispatch_body,
        out_shape=jax.ShapeDtypeStruct((n_rows, D_MODEL), F32),
        grid=(steps,),
        in_specs=[pl.BlockSpec((1, 1, tb * TOP_K), lambda i: (i, 0, 0), memory_space=pltpu.SMEM),
                  pl.BlockSpec((tb, D_MODEL), lambda i: (i, 0)),
                  pl.BlockSpec((1, D_MODEL), lambda i: (0, 0))],
        out_specs=pl.BlockSpec(memory_space=pl.ANY),
        scratch_shapes=[pltpu.VMEM((2, tb, D_MODEL), F32), pltpu.SemaphoreType.DMA((2,))],
        compiler_params=_cparams(("arbitrary",)),
        name="moe_dispatch",
    )(pos3, x1_all, g2)


def _expert_body(sbe_ref, sbs_ref, sbr_ref, xs_ref, wg_ref, wu_ref, wd_ref, bg_ref, bu_ref,
                 bd_ref, ys_ref, x_buf, y_buf, wgu_bf, wd_bf, x_stage, sem_in, sem_out):
    del sbe_ref
    s = pl.program_id(0)
    f = pl.program_id(1)
    n_f = pl.num_programs(1)
    rows = sbr_ref[s]
    start = pl.multiple_of(sbs_ref[s], MOE_SUB)
    nsub = lax.shift_right_logical(rows + (MOE_SUB - 1), int(math.log2(MOE_SUB)))

    def sub_rows(j):
        return pl.ds(pl.multiple_of(j * MOE_SUB, MOE_SUB), MOE_SUB)

    def copy_in(j, sl):
        return pltpu.make_async_copy(xs_ref.at[pl.ds(start + j * MOE_SUB, MOE_SUB)],
                                     x_stage.at[sl], sem_in.at[sl])

    def copy_out(j):
        return pltpu.make_async_copy(y_buf.at[sub_rows(j)],
                                     ys_ref.at[pl.ds(start + j * MOE_SUB, MOE_SUB)],
                                     sem_out.at[0])

    @pl.when(rows > 0)
    def _():
        wgu_bf[:, 0:MOE_FF] = wg_ref[0].astype(BF16)
        wgu_bf[:, MOE_FF:] = wu_ref[0].astype(BF16)
        wd_bf[...] = wd_ref[0].astype(BF16)

        @pl.when(f == 0)
        def _():
            copy_in(0, 0).start()

            def load(j, carry):
                sl = lax.rem(j, 2)

                @pl.when(j + 1 < nsub)
                def _():
                    copy_in(j + 1, 1 - sl).start()

                copy_in(j, sl).wait()
                rid = lax.broadcasted_iota(I32, (MOE_SUB, D_MODEL), 0)
                xv = jnp.where(rid < rows - j * MOE_SUB, x_stage[sl], 0.0)
                x_buf[sub_rows(j), :] = xv.astype(BF16)
                return carry

            lax.fori_loop(0, nsub, load, 0)

        def partial_out(j):
            gu = _dot(x_buf[sub_rows(j), :], wgu_bf[...])
            g = jnp.minimum(gu[:, 0:MOE_FF] + bg_ref[0], SWIGLU_LIMIT)
            u = jnp.clip(gu[:, MOE_FF:] + bu_ref[0], -SWIGLU_LIMIT, SWIGLU_LIMIT)
            a = (u + 1.0) * (g * _sigmoid(SWIGLU_ALPHA * g))
            return _dot(a.astype(BF16), wd_bf[...])

        @pl.when(f == 0)
        def _():
            def first(j, carry):
                y_buf[sub_rows(j), :] = partial_out(j) + bd_ref[0]
                return carry
            lax.fori_loop(0, nsub, first, 0)

        @pl.when(f > 0)
        def _():
            def more(j, carry):
                y_buf[sub_rows(j), :] = y_buf[sub_rows(j), :] + partial_out(j)
                return carry
            lax.fori_loop(0, nsub, more, 0)

        @pl.when(f == n_f - 1)
        def _():
            def put(j, carry):
                copy_out(j).start()
                return carry
            lax.fori_loop(0, nsub, put, 0)

            def drain(j, carry):
                copy_out(j).wait()
                return carry
            lax.fori_loop(0, nsub, drain, 0)


def _experts(xs, sb_e, sb_start, sb_rows, w_gate_up, b_gate_up, w_down, b_down):
    n_rows = xs.shape[0]
    n_sb = sb_e.shape[0]
    n_f = D_FF // MOE_FF

    def fsel(s, f, sbr):
        return jnp.where(sbr[s] > 0, f, n_f - 1)

    wg = pl.BlockSpec((1, D_MODEL, MOE_FF), lambda s, f, e, st, r: (e[s], 0, fsel(s, f, r)))
    wu = pl.BlockSpec((1, D_MODEL, MOE_FF), lambda s, f, e, st, r: (e[s], 0, n_f + fsel(s, f, r)))
    wd = pl.BlockSpec((1, MOE_FF, D_MODEL), lambda s, f, e, st, r: (e[s], fsel(s, f, r), 0))
    bg = pl.BlockSpec((1, 1, MOE_FF), lambda s, f, e, st, r: (e[s], 0, fsel(s, f, r)))
    bu = pl.BlockSpec((1, 1, MOE_FF), lambda s, f, e, st, r: (e[s], 0, n_f + fsel(s, f, r)))
    bd = pl.BlockSpec((1, 1, D_MODEL), lambda s, f, e, st, r: (e[s], 0, 0))
    grid_spec = pltpu.PrefetchScalarGridSpec(
        num_scalar_prefetch=3,
        grid=(n_sb, n_f),
        in_specs=[pl.BlockSpec(memory_space=pl.ANY), wg, wu, wd, bg, bu, bd],
        out_specs=pl.BlockSpec(memory_space=pl.ANY),
        scratch_shapes=[pltpu.VMEM((MOE_CAP, D_MODEL), BF16),
                        pltpu.VMEM((MOE_CAP, D_MODEL), F32),
                        pltpu.VMEM((D_MODEL, 2 * MOE_FF), BF16),
                        pltpu.VMEM((MOE_FF, D_MODEL), BF16),
                        pltpu.VMEM((2, MOE_SUB, D_MODEL), F32),
                        pltpu.SemaphoreType.DMA((2,)),
                        pltpu.SemaphoreType.DMA((1,))],
    )
    return pl.pallas_call(
        _expert_body,
        out_shape=jax.ShapeDtypeStruct((n_rows, D_MODEL), F32),
        grid_spec=grid_spec,
        compiler_params=_cparams(("arbitrary", "arbitrary"), 60),
        name="moe_experts",
    )(sb_e, sb_start, sb_rows, xs, w_gate_up, w_gate_up, w_down,
      b_gate_up.reshape(N_EXPERTS, 1, 2 * D_FF), b_gate_up.reshape(N_EXPERTS, 1, 2 * D_FF),
      b_down.reshape(N_EXPERTS, 1, D_MODEL))


def _combine_body(pos_ref, gw_ref, x1_ref, ys_ref, o_ref, ybuf, sem):
    tb = x1_ref.shape[0]

    def body(t, carry):
        for k in range(TOP_K):
            p = pos_ref[0, 0, t * TOP_K + k]
            pltpu.make_async_copy(ys_ref.at[pl.ds(p, 1)], ybuf.at[k, pl.ds(t, 1)],
                                  sem.at[0]).start()
        return carry

    lax.fori_loop(0, tb, body, 0)
    pltpu.make_async_copy(ybuf, ybuf, sem.at[0]).wait()
    gw = gw_ref[...]
    y = x1_ref[...]
    for k in range(TOP_K):
        y = y + gw[:, k:k + 1] * ybuf[k]
    o_ref[...] = y


def kernel(x_prompt, x_sample, cache_k, cache_v, state_conv, page_table, norm1_g, w_in, conv_dw_w,
           conv_dw_b, conv_ln_g, conv_ln_b, w_conv_out, q_norm_g, k_norm_g, lambda_q1, lambda_k1,
           lambda_q2, lambda_k2, subln_g, w_attn_out, w_out, norm2_g, router_w, router_b,
           w_gate_up, b_gate_up, w_down, b_down):
    nb, seq, _ = x_prompt.shape
    db, t_new, _ = x_sample.shape
    n_p, n_s = nb * seq, db * t_new
    n_all = n_p + n_s
    assert n_all % ROW_TILE == 0 and n_all % DISPATCH_TILE == 0 and n_p % n_s == 0
    assert w_in.shape[0] == 1

    row = lambda a: a[0].reshape(1, -1).astype(F32)
    w_in_b = w_in[0].astype(BF16)
    wc_b = w_conv_out[0].astype(BF16)
    wa_b = w_attn_out[0].astype(BF16)
    wo_b = w_out[0].astype(BF16)
    rw = jnp.pad(router_w[0], ((0, 0), (0, LANES - N_EXPERTS)))
    rwh = rw.astype(BF16)
    rwl = (rw - rwh.astype(F32)).astype(BF16)
    rb = jnp.pad(router_b[0].astype(F32), (0, LANES - N_EXPERTS)).reshape(1, LANES)
    gq = jnp.tile(q_norm_g[0], QKV_W // DIFF_HD).reshape(1, QKV_W)
    gk = jnp.tile(k_norm_g[0], QKV_W // DIFF_HD).reshape(1, QKV_W)
    blk = jnp.arange(MXU_DIM) // DIFF_HD
    gmat = (blk[:, None] == blk[None, :]).astype(BF16)
    lam_p = jnp.stack([lambda_q1[0], lambda_k1[0], lambda_q2[0], lambda_k2[0]]).astype(F32)
    dw_w = jnp.pad(conv_dw_w[0], ((0, CONV_HALO - CONV_W), (0, 0)))

    xp = x_prompt.reshape(n_p, D_MODEL)
    xs_ = x_sample.reshape(n_s, D_MODEL)
    h_all = _rms_into(xp, row(norm1_g), n_all, 512, 0)
    h_all = _rms_into(xs_, row(norm1_g), n_all, n_s, n_p // n_s, prev=h_all)
    u_all = _glu_proj(h_all, w_in_b)
    q_all, k_all, kb_all, v_all, vb_all = _qkv_proj(h_all, w_in_b, gq, gk, gmat)
    sig_all = _gate_proj(h_all, w_in_b)

    conv_args = (dw_w, row(conv_dw_b), row(conv_ln_g), row(conv_ln_b))
    c_p = _conv_prompt(u_all, nb, seq, *conv_args)
    u_s = u_all[n_p:].reshape(db, t_new, C_CONV)
    ext_s = jnp.concatenate([state_conv[0], u_s], axis=1)
    c_s = _conv_sample(ext_s, *conv_args).reshape(n_s, C_CONV)
    conv_prompt_new = u_all[:n_p].reshape(nb, seq, C_CONV)[:, seq - (CONV_W - 1):][None]
    conv_sample_new = ext_s[:, t_new:][None]

    o_p = _attn_prompt(q_all, kb_all, vb_all, nb, seq, lam_p, row(subln_g))
    q_s = q_all[n_p:].reshape(db, t_new, N_HEADS, 2, DIFF_HD)
    qcols = jnp.zeros((db, 2, DIFF_HD, 2, N_HEADS, t_new), BF16)
    for m in range(2):
        qcols = qcols.at[:, m, :, m].set(jnp.transpose(q_s[:, :, :, m, :], (0, 3, 2, 1)))
    qcols = qcols.reshape(db, HEAD_W, 2 * N_HEADS * t_new)
    qcols = jnp.pad(qcols, ((0, 0), (0, 0), (0, LANES - 2 * N_HEADS * t_new)))
    k_new = kb_all[n_p:].reshape(db, t_new * N_HEADS, HEAD_W)
    v_new = vb_all[n_p:].reshape(db, t_new * N_HEADS, HEAD_W)
    o_s = _attn_sample(qcols, cache_k[0], cache_v[0], page_table, k_new, v_new, lam_p,
                       row(subln_g), t_new)
    o_s = jnp.transpose(o_s.reshape(db, N_HEADS, t_new, HEAD_W), (0, 2, 1, 3)).reshape(n_s, QKV_W)

    merge_w = (wc_b, wa_b, wo_b, row(norm2_g), rwh, rwl, rb)
    x1_all, idx_p, gw_p = _merge(xp, c_p, o_p, sig_all, *merge_w, n_all, MERGE_TILE, 0)
    x1_all, idx_s, gw_s = _merge(xs_, c_s, o_s, sig_all, *merge_w, n_all, n_s, n_p // n_s,
                                 prev=x1_all)
    top_idx = jnp.concatenate([idx_p, idx_s], axis=0)
    gate_w = jnp.concatenate([gw_p, gw_s], axis=0)

    pos, sb_e, sb_start, sb_rows, n_rows = _routing_tables(top_idx)
    xs_sorted = _dispatch(x1_all, row(norm2_g), pos.reshape(-1), n_rows)
    ys = _experts(xs_sorted, sb_e, sb_start, sb_rows, w_gate_up[0], b_gate_up[0], w_down[0],
                  b_down[0])
    pos_flat = pos.reshape(-1)
    y_p = _combine_call(pos_flat, gate_w, x1_all, ys, n_p, COMBINE_TILE, 0)
    y_s = _combine_call(pos_flat, gate_w, x1_all, ys, n_s, n_s, n_p // n_s)

    kv_shape = lambda n, lead: (1, lead, n // lead, N_HEADS, HEAD_W)
    return (y_p.reshape(nb, seq, D_MODEL), y_s.reshape(db, t_new, D_MODEL),
            k_all[:n_p].reshape(kv_shape(n_p, nb)), v_all[:n_p].reshape(kv_shape(n_p, nb)),
            conv_prompt_new,
            k_all[n_p:].reshape(kv_shape(n_s, db)), v_all[n_p:].reshape(kv_shape(n_s, db)),
            conv_sample_new)


def _combine_call(pos_flat, gate_w, x1_all, ys, n, tile, block0):
    row0 = block0 * tile
    steps = n // tile
    pos3 = pos_flat[row0 * TOP_K:(row0 + n) * TOP_K].reshape(steps, 1, tile * TOP_K)
    gw = gate_w[row0:row0 + n]
    return pl.pallas_call(
        _combine_body,
        out_shape=jax.ShapeDtypeStruct((n, D_MODEL), F32),
        grid=(steps,),
        in_specs=[pl.BlockSpec((1, 1, tile * TOP_K), lambda i: (i, 0, 0),
                               memory_space=pltpu.SMEM),
                  pl.BlockSpec((tile, TOP_K), lambda i: (i, 0)),
                  pl.BlockSpec((tile, D_MODEL), lambda i: (block0 + i, 0)),
                  pl.BlockSpec(memory_space=pl.ANY)],
        out_specs=pl.BlockSpec((tile, D_MODEL), lambda i: (i, 0)),
        scratch_shapes=[pltpu.VMEM((TOP_K, tile, D_MODEL), F32), pltpu.SemaphoreType.DMA((1,))],
        compiler_params=_cparams(("arbitrary",), 40),
        name="moe_combine",
    )(pos3, gw, x1_all, ys)
```

```python
import functools
import math

import jax
import jax.numpy as jnp
from jax import lax
from jax.experimental import pallas as pl
from jax.experimental.pallas import tpu as pltpu

F32, BF16, I32 = jnp.float32, jnp.bfloat16, jnp.int32

D_MODEL = 2048
C_CONV = 1024
CONV_W = 31
N_HEADS = 8
DIFF_HD = 64
HEAD_W = 2 * DIFF_HD
QKV_W = N_HEADS * HEAD_W
N_EXPERTS = 32
TOP_K = 4
D_FF = 2048
SWIGLU_LIMIT = 7.0
SWIGLU_ALPHA = 1.702
RMS_EPS = 1e-6
LN_EPS = 1e-5
LAM_INIT = 0.8 - 0.6 * math.exp(-0.3 * 0)
PAGE = 128
MASK_VALUE = -1e30

LANES = 128
MXU_DIM = 256

ROW_TILE = 512
CONV_TILE = 256
CONV_CHUNK = 64
CONV_LANES = 256
CONV_LN_ROWS = 32
CONV_HALO = 32
ATTN_TILE = 512
DEC_PAGES = 4
MERGE_TILE = 256
MOE_SUB = 256
MOE_CAP = 2304
MOE_FF = 256
DISPATCH_TILE = 384
COMBINE_TILE = 256


def _cparams(semantics, vmem_mb=None):
    kw = dict(dimension_semantics=semantics)
    if vmem_mb is not None:
        kw["vmem_limit_bytes"] = vmem_mb * 1024 * 1024
    return pltpu.CompilerParams(**kw)


def _sigmoid(x):
    return 1.0 / (1.0 + jnp.exp(-x))


def _dot(a, b):
    return jnp.dot(a, b, preferred_element_type=F32)


def _rms(x, g):
    ms = jnp.mean(x * x, axis=-1, keepdims=True)
    return x * lax.rsqrt(ms + RMS_EPS) * g


def _dot3(a, w_hi, w_lo):
    a_hi = a.astype(BF16)
    a_lo = (a - a_hi.astype(F32)).astype(BF16)
    return _dot(a_hi, w_hi) + _dot(a_lo, w_hi) + _dot(a_hi, w_lo)


def _split_bf16(w):
    hi = w.astype(BF16)
    return hi, (w - hi.astype(F32)).astype(BF16)


def _rms_body(x_ref, g_ref, o_ref):
    o_ref[...] = _rms(x_ref[...], g_ref[...]).astype(o_ref.dtype)


def _rms_cast(x, g):
    n = x.shape[0]
    return pl.pallas_call(
        _rms_body,
        out_shape=jax.ShapeDtypeStruct((n, D_MODEL), BF16),
        grid=(n // ROW_TILE,),
        in_specs=[pl.BlockSpec((ROW_TILE, D_MODEL), lambda i: (i, 0)),
                  pl.BlockSpec((1, D_MODEL), lambda i: (0, 0))],
        out_specs=pl.BlockSpec((ROW_TILE, D_MODEL), lambda i: (i, 0)),
        compiler_params=_cparams(("arbitrary",)),
        name="rms_in",
    )(x, g)


IN_COLS = 2 * C_CONV + 3 * QKV_W + 2 * D_MODEL


def _sample_inproj_body(x_ref, g_ref, wh_ref, wl_ref, z_ref):
    z_ref[...] = _dot3(_rms(x_ref[...], g_ref[...]), wh_ref[...], wl_ref[...])


def _sample_inproj(x, g, w_hi, w_lo):
    n = x.shape[0]
    wspec = pl.BlockSpec((D_MODEL, QKV_W), lambda j: (0, j))
    return pl.pallas_call(
        _sample_inproj_body,
        out_shape=jax.ShapeDtypeStruct((n, IN_COLS), F32),
        grid=(IN_COLS // QKV_W,),
        in_specs=[pl.BlockSpec((n, D_MODEL), lambda j: (0, 0)),
                  pl.BlockSpec((1, D_MODEL), lambda j: (0, 0)), wspec, wspec],
        out_specs=pl.BlockSpec((n, QKV_W), lambda j: (0, j)),
        compiler_params=_cparams(("arbitrary",)),
        name="sample_inproj",
    )(x, g, w_hi, w_lo)


def _sample_post_body(z_ref, gq_ref, gk_ref, gm_ref, u_ref, q_ref, k_ref, v_ref, sig_ref):
    o1 = 2 * C_CONV
    o2, o3, o4 = o1 + QKV_W, o1 + 2 * QKV_W, o1 + 3 * QKV_W
    gmat = gm_ref[...]
    u_ref[...] = z_ref[:, 0:C_CONV] * _sigmoid(z_ref[:, C_CONV:o1])
    q = _group_rms(z_ref[:, o1:o2], gq_ref[...], gmat)
    q_ref[...] = (q * (DIFF_HD ** -0.5)).astype(BF16)
    k_ref[...] = _group_rms(z_ref[:, o2:o3], gk_ref[...], gmat)
    v_ref[...] = z_ref[:, o3:o4]
    sig_ref[...] = _sigmoid(z_ref[:, o4:])


def _sample_post(z, gq, gk, gmat):
    n = z.shape[0]
    full = lambda a: pl.BlockSpec(a.shape, lambda: (0,) * a.ndim)
    sds = lambda w, dt: jax.ShapeDtypeStruct((n, w), dt)
    outs = (sds(C_CONV, F32), sds(QKV_W, BF16), sds(QKV_W, F32), sds(QKV_W, F32),
            sds(2 * D_MODEL, F32))
    return pl.pallas_call(
        _sample_post_body,
        out_shape=outs,
        in_specs=[full(z), full(gq), full(gk), full(gmat)],
        out_specs=tuple(pl.BlockSpec(o.shape, lambda: (0, 0)) for o in outs),
        name="sample_post",
    )(z, gq, gk, gmat)


def _glu_body(h_ref, wa_ref, wg_ref, u_ref):
    h = h_ref[...]
    a = _dot(h, wa_ref[...])
    g = _dot(h, wg_ref[...])
    u_ref[...] = a * _sigmoid(g)


def _glu_proj(h, w_in):
    n = h.shape[0]
    return pl.pallas_call(
        _glu_body,
        out_shape=jax.ShapeDtypeStruct((n, C_CONV), F32),
        grid=(n // ROW_TILE,),
        in_specs=[pl.BlockSpec((ROW_TILE, D_MODEL), lambda i: (i, 0)),
                  pl.BlockSpec((D_MODEL, C_CONV), lambda i: (0, 0)),
                  pl.BlockSpec((D_MODEL, C_CONV), lambda i: (0, 1))],
        out_specs=pl.BlockSpec((ROW_TILE, C_CONV), lambda i: (i, 0)),
        compiler_params=_cparams(("arbitrary",), 48),
        name="glu_proj",
    )(h, w_in, w_in)


def _group_rms(z, g, gmat):
    zz = z * z
    hi = zz.astype(BF16)
    lo = (zz - hi.astype(F32)).astype(BF16)
    cols = []
    for c in range(QKV_W // MXU_DIM):
        sl = slice(c * MXU_DIM, (c + 1) * MXU_DIM)
        cols.append(_dot(hi[:, sl], gmat) + _dot(lo[:, sl], gmat))
    ss = jnp.concatenate(cols, axis=1)
    return z * lax.rsqrt(ss * (1.0 / DIFF_HD) + RMS_EPS) * g


def _qkv_body(h_ref, wq_ref, wk_ref, wv_ref, gq_ref, gk_ref, gm_ref,
              q_ref, kf_ref, kb_ref, vf_ref, vb_ref):
    h = h_ref[...]
    gmat = gm_ref[...]
    q = _group_rms(_dot(h, wq_ref[...]), gq_ref[...], gmat)
    q_ref[...] = (q * (DIFF_HD ** -0.5)).astype(BF16)
    k = _group_rms(_dot(h, wk_ref[...]), gk_ref[...], gmat)
    kf_ref[...] = k
    kb_ref[...] = k.astype(BF16)
    v = _dot(h, wv_ref[...])
    vf_ref[...] = v
    vb_ref[...] = v.astype(BF16)


def _qkv_proj(h, w_in, gq, gk, gmat):
    n = h.shape[0]
    wspec = lambda j: pl.BlockSpec((D_MODEL, QKV_W), lambda i: (0, j))
    row = lambda w: pl.BlockSpec((ROW_TILE, w), lambda i: (i, 0))
    const = lambda r, c: pl.BlockSpec((r, c), lambda i: (0, 0))
    sds = lambda dt: jax.ShapeDtypeStruct((n, QKV_W), dt)
    return pl.pallas_call(
        _qkv_body,
        out_shape=(sds(BF16), sds(F32), sds(BF16), sds(F32), sds(BF16)),
        grid=(n // ROW_TILE,),
        in_specs=[row(D_MODEL), wspec(2), wspec(3), wspec(4),
                  const(1, QKV_W), const(1, QKV_W), const(MXU_DIM, MXU_DIM)],
        out_specs=(row(QKV_W),) * 5,
        compiler_params=_cparams(("arbitrary",), 56),
        name="qkv_proj",
    )(h, w_in, w_in, w_in, gq, gk, gmat)


def _gate_body(h_ref, w_ref, o_ref):
    o_ref[...] = _sigmoid(_dot(h_ref[...], w_ref[...]))


def _gate_proj(h, w_in):
    n = h.shape[0]
    first = (2 * C_CONV + 3 * QKV_W) // QKV_W
    return pl.pallas_call(
        _gate_body,
        out_shape=jax.ShapeDtypeStruct((n, 2 * D_MODEL), F32),
        grid=(2 * D_MODEL // QKV_W, n // ROW_TILE),
        in_specs=[pl.BlockSpec((ROW_TILE, D_MODEL), lambda j, i: (i, 0)),
                  pl.BlockSpec((D_MODEL, QKV_W), lambda j, i: (0, first + j))],
        out_specs=pl.BlockSpec((ROW_TILE, QKV_W), lambda j, i: (i, j)),
        compiler_params=_cparams(("arbitrary", "arbitrary"), 40),
        name="gate_proj",
    )(h, w_in)


def _ln_swish(c, g, b):
    mu = jnp.mean(c, axis=-1, keepdims=True)
    d = c - mu
    var = jnp.mean(d * d, axis=-1, keepdims=True)
    y = d * lax.rsqrt(var + LN_EPS) * g + b
    return y * _sigmoid(y)


def _conv_body(u_ref, w_ref, b_ref, g_ref, beta_ref, o_ref, ext_ref, c_ref):
    i = pl.program_id(1)

    @pl.when(i == 0)
    def _():
        ext_ref[0:CONV_HALO, :] = jnp.zeros((CONV_HALO, C_CONV), F32)

    @pl.when(i > 0)
    def _():
        ext_ref[0:CONV_HALO, :] = ext_ref[CONV_TILE:CONV_TILE + CONV_HALO, :]

    ext_ref[CONV_HALO:, :] = u_ref[...]
    lead = CONV_HALO - (CONV_W - 1)
    for r0 in range(0, CONV_TILE, CONV_CHUNK):
        for c0 in range(0, C_CONV, CONV_LANES):
            cols = slice(c0, c0 + CONV_LANES)
            out = None
            for b in range(8):
                rows = CONV_CHUNK + (8 if b else 0)
                part = None
                for a in range((lead + CONV_W + 7) // 8):
                    jp = 8 * a + b
                    if jp < lead or jp >= lead + CONV_W:
                        continue
                    term = ext_ref[r0 + 8 * a:r0 + 8 * a + rows, cols] * w_ref[jp:jp + 1, cols]
                    part = term if part is None else part + term
                if part is None:
                    continue
                shifted = part[b:b + CONV_CHUNK, :]
                out = shifted if out is None else out + shifted
            c_ref[r0:r0 + CONV_CHUNK, cols] = out + b_ref[:, cols]
    for r0 in range(0, CONV_TILE, CONV_LN_ROWS):
        y = _ln_swish(c_ref[r0:r0 + CONV_LN_ROWS, :], g_ref[...], beta_ref[...])
        o_ref[r0:r0 + CONV_LN_ROWS, :] = y.astype(o_ref.dtype)


def _conv_prompt(u_all, n_batch, seq, dw_w, dw_b, ln_g, ln_b):
    nt = seq // CONV_TILE
    const = lambda r: pl.BlockSpec((r, C_CONV), lambda b, i: (0, 0))
    lead = CONV_HALO - (CONV_W - 1)
    taps = jnp.pad(dw_w[:CONV_W], ((lead, -(lead + CONV_W) % 8), (0, 0)))
    return pl.pallas_call(
        _conv_body,
        out_shape=jax.ShapeDtypeStruct((n_batch * seq, C_CONV), BF16),
        grid=(n_batch, nt),
        in_specs=[pl.BlockSpec((CONV_TILE, C_CONV), lambda b, i: (b * nt + i, 0)),
                  const(taps.shape[0]), const(1), const(1), const(1)],
        out_specs=pl.BlockSpec((CONV_TILE, C_CONV), lambda b, i: (b * nt + i, 0)),
        scratch_shapes=[pltpu.VMEM((CONV_HALO + CONV_TILE, C_CONV), F32),
                        pltpu.VMEM((CONV_TILE, C_CONV), F32)],
        compiler_params=_cparams(("arbitrary", "arbitrary")),
        name="conv_prompt",
    )(u_all, taps, dw_b, ln_g, ln_b)


def _conv_sample_body(ext_ref, w_ref, b_ref, g_ref, beta_ref, o_ref, *, t_new):
    acc = jnp.zeros((t_new, C_CONV), F32) + b_ref[...]
    for j in range(CONV_W):
        acc = acc + ext_ref[0, j:j + t_new, :] * w_ref[j:j + 1, :]
    y = _ln_swish(acc, g_ref[...], beta_ref[...])
    o_ref[0] = y.astype(o_ref.dtype)


def _conv_sample(ext, dw_w, dw_b, ln_g, ln_b):
    nb, rows, _ = ext.shape
    t_new = rows - (CONV_W - 1)
    const = lambda r: pl.BlockSpec((r, C_CONV), lambda b: (0, 0))
    return pl.pallas_call(
        functools.partial(_conv_sample_body, t_new=t_new),
        out_shape=jax.ShapeDtypeStruct((nb, t_new, C_CONV), F32),
        grid=(nb,),
        in_specs=[pl.BlockSpec((1, rows, C_CONV), lambda b: (b, 0, 0)),
                  const(CONV_HALO), const(1), const(1), const(1)],
        out_specs=pl.BlockSpec((1, t_new, C_CONV), lambda b: (b, 0, 0)),
        compiler_params=_cparams(("arbitrary",)),
        name="conv_sample",
    )(ext, dw_w, dw_b, ln_g, ln_b)


def _lambda_value(lam_ref):
    lp = lam_ref[...]
    a = jnp.sum(lp[0:1, :] * lp[1:2, :], axis=-1, keepdims=True)
    b = jnp.sum(lp[2:3, :] * lp[3:4, :], axis=-1, keepdims=True)
    return jnp.exp(a) - jnp.exp(b) + LAM_INIT


def _subln(o, g):
    return _rms(o, g) * (1.0 - LAM_INIT)


def _attn_body(qt_ref, k_ref, vt_ref, lam_ref, g_ref, ot_ref, m_ref, l_ref, acc_ref):
    nq, _, t = qt_ref.shape
    lam = _lambda_value(lam_ref)
    lane = lax.broadcasted_iota(I32, (t, HEAD_W), 1)
    key_row = lax.broadcasted_iota(I32, (t, t), 0)
    query_col = lax.broadcasted_iota(I32, (t, t), 1)
    causal = key_row <= query_col

    for qi in range(nq):
        qt = qt_ref[qi]
        m_ref[...] = jnp.full(m_ref.shape, MASK_VALUE, F32)
        l_ref[...] = jnp.zeros(l_ref.shape, F32)
        acc_ref[...] = jnp.zeros(acc_ref.shape, F32)

        def tile(ki, diagonal, qt=qt):
            k = k_ref[pl.ds(pl.multiple_of(ki * t, t), t), :]
            vt = vt_ref[ki]
            zero = jnp.zeros_like(k)
            for m in range(2):
                in_map = (lane < DIFF_HD) if m == 0 else (lane >= DIFF_HD)
                st = _dot(jnp.where(in_map, k, zero), qt)
                if diagonal:
                    st = jnp.where(causal, st, MASK_VALUE)
                m_prev = m_ref[m]
                m_new = jnp.maximum(m_prev, jnp.max(st, axis=0, keepdims=True))
                alpha = jnp.exp(m_prev - m_new)
                p = jnp.exp(st - m_new)
                l_ref[m] = alpha * l_ref[m] + jnp.sum(p, axis=0, keepdims=True)
                acc_ref[m] = alpha * acc_ref[m] + _dot(vt, p.astype(BF16))
                m_ref[m] = m_new

        if qi > 0:
            def off_diagonal(ki, carry):
                tile(ki, False)
                return carry
            lax.fori_loop(0, qi, off_diagonal, 0)
        tile(qi, True)

        o = acc_ref[0] / l_ref[0] - lam * (acc_ref[1] / l_ref[1])
        ms = jnp.mean(o * o, axis=0, keepdims=True)
        o = o * lax.rsqrt(ms + RMS_EPS) * g_ref[...] * (1.0 - LAM_INIT)
        ot_ref[qi] = o.astype(ot_ref.dtype)


def _attn_prompt(qt, k_all, vt, n_batch, seq, lam_p, subln_col):
    nq = seq // ATTN_TILE
    tspec = pl.BlockSpec((nq, HEAD_W, ATTN_TILE), lambda b, h: (b, h, 0))
    return pl.pallas_call(
        _attn_body,
        out_shape=jax.ShapeDtypeStruct(qt.shape, BF16),
        grid=(n_batch, N_HEADS),
        in_specs=[tspec, pl.BlockSpec((seq, HEAD_W), lambda b, h: (b, h)), tspec,
                  pl.BlockSpec((4, DIFF_HD), lambda b, h: (0, 0)),
                  pl.BlockSpec((HEAD_W, 1), lambda b, h: (0, 0))],
        out_specs=tspec,
        scratch_shapes=[pltpu.VMEM((2, 1, ATTN_TILE), F32),
                        pltpu.VMEM((2, 1, ATTN_TILE), F32),
                        pltpu.VMEM((2, HEAD_W, ATTN_TILE), F32)],
        compiler_params=_cparams(("arbitrary", "arbitrary")),
        name="attn_prompt",
    )(qt, k_all, vt, lam_p, subln_col)


def _to_feature_major(x, n_rows):
    return jnp.transpose(x[:n_rows].reshape(n_rows // ATTN_TILE, ATTN_TILE, x.shape[1]), (0, 2, 1))


def _dec_body(pt_ref, q_ref, *refs, t_new):
    del pt_ref
    k_refs = refs[:DEC_PAGES]
    v_refs = refs[DEC_PAGES:2 * DEC_PAGES]
    knew_ref, vnew_ref, lam_ref, g_ref, o_ref, m_ref, l_ref, acc_ref = refs[2 * DEC_PAGES:]
    p = pl.program_id(1)
    page_rows = PAGE * N_HEADS
    half = N_HEADS * t_new

    @pl.when(p == 0)
    def _():
        m_ref[...] = jnp.full(m_ref.shape, MASK_VALUE, F32)
        l_ref[...] = jnp.zeros(l_ref.shape, F32)
        acc_ref[...] = jnp.zeros(acc_ref.shape, F32)

    qrows = q_ref[0]

    def head_mask(n_keys):
        r = lax.broadcasted_iota(I32, (2 * half, n_keys), 0)
        c = lax.broadcasted_iota(I32, (2 * half, n_keys), 1)
        return r, c, (c % N_HEADS) == ((r % half) // t_new)

    def update(kbs, vbs, keep):
        scores = [jnp.where(keep, lax.dot_general(qrows, kb, (((1,), (1,)), ((), ())),
                                                  preferred_element_type=F32), MASK_VALUE)
                  for kb in kbs]
        m_prev = m_ref[...]
        m_new = m_prev
        for s in scores:
            m_new = jnp.maximum(m_new, jnp.max(s, axis=1, keepdims=True))
        alpha = jnp.exp(m_prev - m_new)
        l_new = alpha * l_ref[...]
        acc = alpha * acc_ref[...]
        for s, vb in zip(scores, vbs):
            pr = jnp.exp(s - m_new)
            l_new = l_new + jnp.sum(pr, axis=1, keepdims=True)
            acc = acc + _dot(pr.astype(BF16), vb)
        l_ref[...] = l_new
        acc_ref[...] = acc
        m_ref[...] = m_new

    _, _, keep_page = head_mask(page_rows)
    update([k_refs[j][0].astype(BF16) for j in range(DEC_PAGES)],
           [v_refs[j][0].astype(BF16) for j in range(DEC_PAGES)], keep_page)

    @pl.when(p == pl.num_programs(1) - 1)
    def _():
        r, c, keep = head_mask(knew_ref.shape[1])
        causal = ((c // N_HEADS) <= (r % t_new)) & (c < half)
        update([knew_ref[0]], [vnew_ref[0]], keep & causal)
        o_maps = acc_ref[...] / l_ref[...]
        lam = _lambda_value(lam_ref)
        o = o_maps[0:half, :] - lam * o_maps[half:2 * half, :]
        o_ref[0] = _subln(o, g_ref[...]).astype(o_ref.dtype)


def _attn_sample(qcols, cache_k, cache_v, page_table, k_new, v_new, lam_p, subln_g, t_new):
    nb = qcols.shape[0]
    n_pages = page_table.shape[1]
    steps = n_pages // DEC_PAGES
    page_rows = PAGE * N_HEADS
    ck = cache_k.reshape(cache_k.shape[0], page_rows, HEAD_W)
    cv = cache_v.reshape(cache_v.shape[0], page_rows, HEAD_W)
    pt = page_table.reshape(-1)

    def page_spec(j):
        return pl.BlockSpec((1, page_rows, HEAD_W),
                            lambda b, p, pt_ref: (pt_ref[b * n_pages + p * DEC_PAGES + j], 0, 0))

    per_b = lambda r, c: pl.BlockSpec((1, r, c), lambda b, p, pt_ref: (b, 0, 0))
    const = lambda r, c: pl.BlockSpec((r, c), lambda b, p, pt_ref: (0, 0))
    rows_new = t_new * N_HEADS
    q_rows = 2 * rows_new
    grid_spec = pltpu.PrefetchScalarGridSpec(
        num_scalar_prefetch=1,
        grid=(nb, steps),
        in_specs=[per_b(q_rows, HEAD_W)]
                 + [page_spec(j) for j in range(DEC_PAGES)] * 2
                 + [per_b(k_new.shape[1], HEAD_W), per_b(v_new.shape[1], HEAD_W),
                    const(4, DIFF_HD), const(1, HEAD_W)],
        out_specs=per_b(rows_new, HEAD_W),
        scratch_shapes=[pltpu.VMEM((q_rows, 1), F32), pltpu.VMEM((q_rows, 1), F32),
                        pltpu.VMEM((q_rows, HEAD_W), F32)],
    )
    return pl.pallas_call(
        functools.partial(_dec_body, t_new=t_new),
        out_shape=jax.ShapeDtypeStruct((nb, rows_new, HEAD_W), F32),
        grid_spec=grid_spec,
        compiler_params=_cparams(("arbitrary", "arbitrary")),
        name="attn_sample",
    )(pt, qcols, *([ck] * DEC_PAGES), *([cv] * DEC_PAGES), k_new, v_new, lam_p, subln_g)


def _route(x1, g2_ref, rwh_ref, rwl_ref, rb_ref, idx_ref, gw_ref):
    tm = x1.shape[0]
    logits = _dot3(_rms(x1, g2_ref[...]), rwh_ref[...], rwl_ref[...]) + rb_ref[...]
    lane = lax.broadcasted_iota(I32, (tm, LANES), 1)
    v = jnp.where(lane < N_EXPERTS, logits, -jnp.inf)
    kcol = lax.broadcasted_iota(I32, (tm, TOP_K), 1)
    idx_out = jnp.zeros((tm, TOP_K), I32)
    val_out = jnp.zeros((tm, TOP_K), F32)
    for k in range(TOP_K):
        mx = jnp.max(v, axis=1, keepdims=True)
        ix = jnp.min(jnp.where(v == mx, lane, LANES), axis=1, keepdims=True)
        idx_out = jnp.where(kcol == k, ix, idx_out)
        val_out = jnp.where(kcol == k, mx, val_out)
        v = jnp.where(lane == ix, -jnp.inf, v)
    e = jnp.exp(val_out - val_out[:, 0:1])
    idx_ref[...] = idx_out
    gw_ref[...] = e / jnp.sum(e, axis=1, keepdims=True)


def _merge_body(x_ref, c_ref, o_ref, sc_ref, sa_ref, wc_ref, wa_ref, wo_ref, g2_ref,
                rwh_ref, rwl_ref, rb_ref, x1_ref, idx_ref, gw_ref):
    c_out = _dot(c_ref[...], wc_ref[...])
    a_out = _dot(o_ref[...], wa_ref[...])
    merged = sc_ref[...] * c_out + sa_ref[...] * a_out
    x1 = x_ref[...] + _dot(merged.astype(BF16), wo_ref[...])
    x1_ref[...] = x1
    _route(x1, g2_ref, rwh_ref, rwl_ref, rb_ref, idx_ref, gw_ref)


def _merge(x, c_act, o_act, sig, wc, wa, wo, g2, rwh, rwl, rb, total_rows):
    n = x.shape[0]
    tile = MERGE_TILE
    row = lambda w: pl.BlockSpec((tile, w), lambda i: (i, 0))
    const = lambda r, c: pl.BlockSpec((r, c), lambda i: (0, 0))
    return pl.pallas_call(
        _merge_body,
        out_shape=(jax.ShapeDtypeStruct((total_rows, D_MODEL), F32),
                   jax.ShapeDtypeStruct((n, TOP_K), I32),
                   jax.ShapeDtypeStruct((n, TOP_K), F32)),
        grid=(n // tile,),
        in_specs=[row(D_MODEL), row(C_CONV), row(QKV_W),
                  pl.BlockSpec((tile, D_MODEL), lambda i: (i, 0)),
                  pl.BlockSpec((tile, D_MODEL), lambda i: (i, 1)),
                  const(C_CONV, D_MODEL), const(QKV_W, D_MODEL), const(D_MODEL, D_MODEL),
                  const(1, D_MODEL), const(D_MODEL, LANES), const(D_MODEL, LANES),
                  const(1, LANES)],
        out_specs=(row(D_MODEL), row(TOP_K), row(TOP_K)),
        compiler_params=_cparams(("arbitrary",), 60),
        name="merge_router",
    )(x, c_act, o_act, sig, sig, wc, wa, wo, g2, rwh, rwl, rb)


SAMPLE_COLS = 512


def _sample_mix_body(c_ref, o_ref, sc_ref, sa_ref, wch_ref, wcl_ref, wah_ref, wal_ref, m_ref):
    m_ref[...] = (sc_ref[...] * _dot3(c_ref[...], wch_ref[...], wcl_ref[...])
                  + sa_ref[...] * _dot3(o_ref[...], wah_ref[...], wal_ref[...]))


def _sample_mix(c_act, o_act, sig, wc, wa):
    n = c_act.shape[0]
    nblk = D_MODEL // SAMPLE_COLS
    full = lambda w: pl.BlockSpec((n, w), lambda j: (0, 0))
    wspec = pl.BlockSpec((C_CONV, SAMPLE_COLS), lambda j: (0, j))
    return pl.pallas_call(
        _sample_mix_body,
        out_shape=jax.ShapeDtypeStruct((n, D_MODEL), F32),
        grid=(nblk,),
        in_specs=[full(C_CONV), full(QKV_W),
                  pl.BlockSpec((n, SAMPLE_COLS), lambda j: (0, j)),
                  pl.BlockSpec((n, SAMPLE_COLS), lambda j: (0, nblk + j)),
                  wspec, wspec, wspec, wspec],
        out_specs=pl.BlockSpec((n, SAMPLE_COLS), lambda j: (0, j)),
        compiler_params=_cparams(("arbitrary",)),
        name="sample_mix",
    )(c_act, o_act, sig, sig, wc[0], wc[1], wa[0], wa[1])


def _sample_out_body(x_ref, m_ref, woh_ref, wol_ref, g2_ref, rwh_ref, rwl_ref, rb_ref, prev_ref,
                     x1_ref, idx_ref, gw_ref, acc_ref):
    del prev_ref
    kb = pl.program_id(0)

    @pl.when(kb == 0)
    def _():
        acc_ref[...] = x_ref[...]

    acc_ref[...] = acc_ref[...] + _dot3(m_ref[...], woh_ref[...], wol_ref[...])

    @pl.when(kb == pl.num_programs(0) - 1)
    def _():
        x1 = acc_ref[...]
        x1_ref[...] = x1
        _route(x1, g2_ref, rwh_ref, rwl_ref, rb_ref, idx_ref, gw_ref)


def _sample_out(x, merged, wo, g2, rwh, rwl, rb, x1_prompt):
    n = x.shape[0]
    total_rows = x1_prompt.shape[0]
    assert (total_rows - n) % n == 0
    const = lambda r, c: pl.BlockSpec((r, c), lambda k: (0, 0))
    wspec = pl.BlockSpec((SAMPLE_COLS, D_MODEL), lambda k: (k, 0))
    return pl.pallas_call(
        _sample_out_body,
        out_shape=(jax.ShapeDtypeStruct((total_rows, D_MODEL), F32),
                   jax.ShapeDtypeStruct((n, TOP_K), I32),
                   jax.ShapeDtypeStruct((n, TOP_K), F32)),
        grid=(D_MODEL // SAMPLE_COLS,),
        in_specs=[const(n, D_MODEL), pl.BlockSpec((n, SAMPLE_COLS), lambda k: (0, k)),
                  wspec, wspec, const(1, D_MODEL), const(D_MODEL, LANES), const(D_MODEL, LANES),
                  const(1, LANES), pl.BlockSpec(memory_space=pl.ANY)],
        out_specs=(pl.BlockSpec((n, D_MODEL), lambda k: ((total_rows - n) // n, 0)),
                   const(n, TOP_K), const(n, TOP_K)),
        scratch_shapes=[pltpu.VMEM((n, D_MODEL), F32)],
        input_output_aliases={8: 0},
        compiler_params=_cparams(("arbitrary",)),
        name="sample_out",
    )(x, merged, wo[0], wo[1], g2, rwh, rwl, rb, x1_prompt)


def _routing_tables(top_idx):
    n = top_idx.shape[0]
    nk = n * TOP_K
    eids = jnp.arange(N_EXPERTS, dtype=I32)
    onehot = (top_idx[:, :, None] == eids[None, None, :]).astype(I32)
    per_tok = jnp.sum(onehot, axis=1)
    csum = jnp.cumsum(per_tok, axis=0)
    counts = csum[-1]
    rank = jnp.take_along_axis(csum - per_tok, top_idx, axis=1)
    padded = (counts + MOE_SUB - 1) // MOE_SUB * MOE_SUB
    gstart = jnp.cumsum(padded) - padded
    pos = (gstart[top_idx] + rank).astype(I32)

    n_sb_max = nk // MOE_CAP + N_EXPERTS
    nsb = (counts + MOE_CAP - 1) // MOE_CAP
    cum = jnp.cumsum(nsb)
    total = cum[-1]
    s = jnp.arange(n_sb_max, dtype=I32)
    ex = jnp.minimum(jnp.sum((cum[None, :] <= s[:, None]).astype(I32), axis=1), N_EXPERTS - 1)
    ex_last = ex[jnp.maximum(total - 1, 0)]
    valid = s < total
    ex = jnp.where(valid, ex, ex_last)
    local = s - (cum[ex] - nsb[ex])
    rows = jnp.where(valid, jnp.clip(counts[ex] - local * MOE_CAP, 0, MOE_CAP), 0).astype(I32)
    start = jnp.where(valid, gstart[ex] + local * MOE_CAP, 0).astype(I32)
    n_rows = (nk + N_EXPERTS * (MOE_SUB - 1) + MOE_SUB - 1) // MOE_SUB * MOE_SUB
    return pos, ex, start, rows, n_rows


def _dispatch_body(pos_ref, x_ref, g_ref, xs_ref, hbuf, sem):
    i = pl.program_id(0)
    n = pl.num_programs(0)
    tb = x_ref.shape[0]
    slot = lax.rem(i, 2)

    def wait_slot(sl):
        pltpu.make_async_copy(xs_ref.at[pl.ds(0, TOP_K * tb)], xs_ref.at[pl.ds(0, TOP_K * tb)],
                              sem.at[sl]).wait()

    @pl.when(i >= 2)
    def _():
        wait_slot(slot)

    hbuf[slot] = _rms(x_ref[...], g_ref[...])

    def body(t, carry):
        for k in range(TOP_K):
            p = pos_ref[0, 0, t * TOP_K + k]
            pltpu.make_async_copy(hbuf.at[slot, pl.ds(t, 1)], xs_ref.at[pl.ds(p, 1)],
                                  sem.at[slot]).start(priority=k % 2)
        return carry

    lax.fori_loop(0, tb, body, 0, unroll=4)

    @pl.when(i == n - 1)
    def _():
        wait_slot(slot)
        wait_slot(1 - slot)


def _dispatch(x1_all, g2, pos, n_rows):
    n = x1_all.shape[0]
    tb = DISPATCH_TILE
    steps = n // tb
    assert steps >= 2
    pos3 = pos.reshape(steps, 1, tb * TOP_K)
    return pl.pallas_call(
        _dispatch_body,
        out_shape=jax.ShapeDtypeStruct((n_rows, D_MODEL), F32),
        grid=(steps,),
        in_specs=[pl.BlockSpec((1, 1, tb * TOP_K), lambda i: (i, 0, 0), memory_space=pltpu.SMEM),
                  pl.BlockSpec((tb, D_MODEL), lambda i: (i, 0)),
                  pl.BlockSpec((1, D_MODEL), lambda i: (0, 0))],
        out_specs=pl.BlockSpec(memory_space=pl.ANY),
        scratch_shapes=[pltpu.VMEM((2, tb, D_MODEL), F32), pltpu.SemaphoreType.DMA((2,))],
        compiler_params=_cparams(("arbitrary",)),
        name="moe_dispatch",
    )(pos3, x1_all, g2)


def _expert_body(sbe_ref, sbs_ref, sbr_ref, xs_ref, wg_ref, wu_ref, wd_ref, bg_ref, bu_ref,
                 bd_ref, ys_ref, x_buf, y_buf, wgu_bf, wd_bf, a_buf, x_stage, sem_in, sem_out):
    del sbe_ref
    s = pl.program_id(0)
    f = pl.program_id(1)
    n_f = pl.num_programs(1)
    rows = sbr_ref[s]
    start = pl.multiple_of(sbs_ref[s], MOE_SUB)
    nsub = lax.shift_right_logical(rows + (MOE_SUB - 1), int(math.log2(MOE_SUB)))

    def sub_rows(j):
        return pl.ds(pl.multiple_of(j * MOE_SUB, MOE_SUB), MOE_SUB)

    def copy_in(j, sl):
        return pltpu.make_async_copy(xs_ref.at[pl.ds(start + j * MOE_SUB, MOE_SUB)],
                                     x_stage.at[sl], sem_in.at[sl])

    def copy_out(j):
        return pltpu.make_async_copy(y_buf.at[sub_rows(j)],
                                     ys_ref.at[pl.ds(start + j * MOE_SUB, MOE_SUB)],
                                     sem_out.at[0])

    @pl.when(rows > 0)
    def _():
        wgu_bf[:, 0:MOE_FF] = wg_ref[0].astype(BF16)
        wgu_bf[:, MOE_FF:] = wu_ref[0].astype(BF16)
        wd_bf[...] = wd_ref[0].astype(BF16)

        def fetch(j):
            sl = lax.rem(j, 2)
            copy_in(j, sl).wait()
            rid = lax.broadcasted_iota(I32, (MOE_SUB, D_MODEL), 0)
            xv = jnp.where(rid < rows - j * MOE_SUB, x_stage[sl], 0.0)
            x_buf[sub_rows(j), :] = xv.astype(BF16)

            @pl.when(j + 2 < nsub)
            def _():
                copy_in(j + 2, sl).start()

            y_buf[sub_rows(j), :] = jnp.broadcast_to(bd_ref[0], (MOE_SUB, D_MODEL))

        def activate(j):
            gu = _dot(x_buf[sub_rows(j), :], wgu_bf[...])
            g = jnp.minimum(gu[:, 0:MOE_FF] + bg_ref[0], SWIGLU_LIMIT)
            u = jnp.clip(gu[:, MOE_FF:] + bu_ref[0], -SWIGLU_LIMIT, SWIGLU_LIMIT)
            a = (u + 1.0) * (g * _sigmoid(SWIGLU_ALPHA * g))
            a_buf[lax.rem(j, 2)] = a.astype(BF16)

        def project(j):
            y_buf[sub_rows(j), :] = y_buf[sub_rows(j), :] + _dot(a_buf[lax.rem(j, 2)], wd_bf[...])

        def run(first, last):
            if first:
                copy_in(0, 0).start()

                @pl.when(nsub > 1)
                def _():
                    copy_in(1, 1).start()

                fetch(0)
            activate(0)

            def body(j, carry):
                if first:
                    fetch(j)
                activate(j)
                project(j - 1)
                if last:
                    copy_out(j - 1).start()
                return carry

            lax.fori_loop(1, nsub, body, 0)
            project(nsub - 1)
            if last:
                copy_out(nsub - 1).start()

                def drain(j, carry):
                    copy_out(j).wait()
                    return carry

                lax.fori_loop(0, nsub, drain, 0)

        n_f_static = D_FF // MOE_FF

        @pl.when(f == 0)
        def _():
            run(True, n_f_static == 1)

        if n_f_static > 2:
            @pl.when((f > 0) & (f < n_f - 1))
            def _():
                run(False, False)

        if n_f_static > 1:
            @pl.when(f == n_f - 1)
            def _():
                run(False, True)


def _experts(xs, sb_e, sb_start, sb_rows, w_gate_up, b_gate_up, w_down, b_down):
    n_rows = xs.shape[0]
    n_sb = sb_e.shape[0]
    n_f = D_FF // MOE_FF

    def fsel(s, f, sbr):
        return jnp.where(sbr[s] > 0, f, n_f - 1)

    wg = pl.BlockSpec((1, D_MODEL, MOE_FF), lambda s, f, e, st, r: (e[s], 0, fsel(s, f, r)))
    wu = pl.BlockSpec((1, D_MODEL, MOE_FF), lambda s, f, e, st, r: (e[s], 0, n_f + fsel(s, f, r)))
    wd = pl.BlockSpec((1, MOE_FF, D_MODEL), lambda s, f, e, st, r: (e[s], fsel(s, f, r), 0))
    bg = pl.BlockSpec((1, 1, MOE_FF), lambda s, f, e, st, r: (e[s], 0, fsel(s, f, r)))
    bu = pl.BlockSpec((1, 1, MOE_FF), lambda s, f, e, st, r: (e[s], 0, n_f + fsel(s, f, r)))
    bd = pl.BlockSpec((1, 1, D_MODEL), lambda s, f, e, st, r: (e[s], 0, 0))
    grid_spec = pltpu.PrefetchScalarGridSpec(
        num_scalar_prefetch=3,
        grid=(n_sb, n_f),
        in_specs=[pl.BlockSpec(memory_space=pl.ANY), wg, wu, wd, bg, bu, bd],
        out_specs=pl.BlockSpec(memory_space=pl.ANY),
        scratch_shapes=[pltpu.VMEM((MOE_CAP, D_MODEL), BF16),
                        pltpu.VMEM((MOE_CAP, D_MODEL), F32),
                        pltpu.VMEM((D_MODEL, 2 * MOE_FF), BF16),
                        pltpu.VMEM((MOE_FF, D_MODEL), BF16),
                        pltpu.VMEM((2, MOE_SUB, MOE_FF), BF16),
                        pltpu.VMEM((2, MOE_SUB, D_MODEL), F32),
                        pltpu.SemaphoreType.DMA((2,)),
                        pltpu.SemaphoreType.DMA((1,))],
    )
    return pl.pallas_call(
        _expert_body,
        out_shape=jax.ShapeDtypeStruct((n_rows, D_MODEL), F32),
        grid_spec=grid_spec,
        compiler_params=_cparams(("arbitrary", "arbitrary"), 60),
        name="moe_experts",
    )(sb_e, sb_start, sb_rows, xs, w_gate_up, w_gate_up, w_down,
      b_gate_up.reshape(N_EXPERTS, 1, 2 * D_FF), b_gate_up.reshape(N_EXPERTS, 1, 2 * D_FF),
      b_down.reshape(N_EXPERTS, 1, D_MODEL))


def _combine_body(pos_ref, gw_ref, x1_ref, ys_ref, o_ref, ybuf, sem):
    tb = x1_ref.shape[0]

    def body(t, carry):
        for k in range(TOP_K):
            p = pos_ref[0, 0, t * TOP_K + k]
            pltpu.make_async_copy(ys_ref.at[pl.ds(p, 1)], ybuf.at[k, pl.ds(t, 1)],
                                  sem.at[0]).start(priority=k % 2)
        return carry

    lax.fori_loop(0, tb, body, 0, unroll=4)
    pltpu.make_async_copy(ybuf, ybuf, sem.at[0]).wait()
    gw = gw_ref[...]
    y = x1_ref[...]
    for k in range(TOP_K):
        y = y + gw[:, k:k + 1] * ybuf[k]
    o_ref[...] = y


def kernel(x_prompt, x_sample, cache_k, cache_v, state_conv, page_table, norm1_g, w_in, conv_dw_w,
           conv_dw_b, conv_ln_g, conv_ln_b, w_conv_out, q_norm_g, k_norm_g, lambda_q1, lambda_k1,
           lambda_q2, lambda_k2, subln_g, w_attn_out, w_out, norm2_g, router_w, router_b,
           w_gate_up, b_gate_up, w_down, b_down):
    nb, seq, _ = x_prompt.shape
    db, t_new, _ = x_sample.shape
    n_p, n_s = nb * seq, db * t_new
    n_all = n_p + n_s
    assert n_p % ROW_TILE == 0 and n_all % DISPATCH_TILE == 0 and n_p % n_s == 0
    assert w_in.shape[0] == 1

    row = lambda a: a[0].reshape(1, -1).astype(F32)
    w_in_b, w_in_lo = _split_bf16(w_in[0])
    wc_b, wc_lo = _split_bf16(w_conv_out[0])
    wa_b, wa_lo = _split_bf16(w_attn_out[0])
    wo_b, wo_lo = _split_bf16(w_out[0])
    rwh, rwl = _split_bf16(jnp.pad(router_w[0], ((0, 0), (0, LANES - N_EXPERTS))))
    rb = jnp.pad(router_b[0].astype(F32), (0, LANES - N_EXPERTS)).reshape(1, LANES)
    gq = jnp.tile(q_norm_g[0], QKV_W // DIFF_HD).reshape(1, QKV_W)
    gk = jnp.tile(k_norm_g[0], QKV_W // DIFF_HD).reshape(1, QKV_W)
    blk = jnp.arange(MXU_DIM) // DIFF_HD
    gmat = (blk[:, None] == blk[None, :]).astype(BF16)
    lam_p = jnp.stack([lambda_q1[0], lambda_k1[0], lambda_q2[0], lambda_k2[0]]).astype(F32)
    dw_w = jnp.pad(conv_dw_w[0], ((0, CONV_HALO - CONV_W), (0, 0)))

    xp = x_prompt.reshape(n_p, D_MODEL)
    xs_ = x_sample.reshape(n_s, D_MODEL)
    h_p = _rms_cast(xp, row(norm1_g))
    u_p = _glu_proj(h_p, w_in_b)
    q_p, k_p, kb_p, v_p, vb_p = _qkv_proj(h_p, w_in_b, gq, gk, gmat)
    sig_p = _gate_proj(h_p, w_in_b)
    z_s = _sample_inproj(xs_, row(norm1_g), w_in_b, w_in_lo)
    u_s, q_s, k_s, v_s, sig_s = _sample_post(z_s, gq, gk, gmat)

    conv_args = (dw_w, row(conv_dw_b), row(conv_ln_g), row(conv_ln_b))
    c_p = _conv_prompt(u_p, nb, seq, *conv_args)
    ext_s = jnp.concatenate([state_conv[0], u_s.reshape(db, t_new, C_CONV)], axis=1)
    c_s = _conv_sample(ext_s, *conv_args).reshape(n_s, C_CONV)
    conv_prompt_new = u_p.reshape(nb, seq, C_CONV)[:, seq - (CONV_W - 1):][None]
    conv_sample_new = ext_s[:, t_new:][None]

    ot = _attn_prompt(_to_feature_major(q_p, n_p), kb_p, _to_feature_major(vb_p, n_p),
                      nb, seq, lam_p, subln_g[0].reshape(HEAD_W, 1).astype(F32))
    o_p = jnp.transpose(ot, (0, 2, 1)).reshape(n_p, QKV_W)
    q_s = jnp.transpose(q_s.reshape(db, t_new, N_HEADS, HEAD_W), (0, 2, 1, 3))
    q_s = q_s.reshape(db, 1, N_HEADS * t_new, HEAD_W)
    in_map = ((jnp.arange(HEAD_W) // DIFF_HD)[None, :] == jnp.arange(2)[:, None])[None, :, None, :]
    qrows = jnp.where(in_map, q_s, jnp.zeros_like(q_s)).reshape(db, 2 * N_HEADS * t_new, HEAD_W)
    pad_new = ((0, 0), (0, LANES - t_new * N_HEADS), (0, 0))
    k_new = jnp.pad(k_s.astype(BF16).reshape(db, t_new * N_HEADS, HEAD_W), pad_new)
    v_new = jnp.pad(v_s.astype(BF16).reshape(db, t_new * N_HEADS, HEAD_W), pad_new)
    o_s = _attn_sample(qrows, cache_k[0], cache_v[0], page_table, k_new, v_new, lam_p,
                       row(subln_g), t_new)
    o_s = jnp.transpose(o_s.reshape(db, N_HEADS, t_new, HEAD_W), (0, 2, 1, 3)).reshape(n_s, QKV_W)

    route_w = (row(norm2_g), rwh, rwl, rb)
    x1_all, idx_p, gw_p = _merge(xp, c_p, o_p, sig_p, wc_b, wa_b, wo_b, *route_w, n_all)
    merged_s = _sample_mix(c_s, o_s, sig_s, (wc_b, wc_lo), (wa_b, wa_lo))
    x1_all, idx_s, gw_s = _sample_out(xs_, merged_s, (wo_b, wo_lo), *route_w, x1_all)
    top_idx = jnp.concatenate([idx_p, idx_s], axis=0)
    gate_w = jnp.concatenate([gw_p, gw_s], axis=0)

    pos, sb_e, sb_start, sb_rows, n_rows = _routing_tables(top_idx)
    xs_sorted = _dispatch(x1_all, row(norm2_g), pos.reshape(-1), n_rows)
    ys = _experts(xs_sorted, sb_e, sb_start, sb_rows, w_gate_up[0], b_gate_up[0], w_down[0],
                  b_down[0])
    pos_flat = pos.reshape(-1)
    y_p = _combine_call(pos_flat, gate_w, x1_all, ys, n_p, COMBINE_TILE, 0)
    y_s = _combine_call(pos_flat, gate_w, x1_all, ys, n_s, n_s, n_p // n_s)

    kv_shape = lambda n, lead: (1, lead, n // lead, N_HEADS, HEAD_W)
    return (y_p.reshape(nb, seq, D_MODEL), y_s.reshape(db, t_new, D_MODEL),
            k_p.reshape(kv_shape(n_p, nb)), v_p.reshape(kv_shape(n_p, nb)), conv_prompt_new,
            k_s.reshape(kv_shape(n_s, db)), v_s.reshape(kv_shape(n_s, db)), conv_sample_new)


def _combine_call(pos_flat, gate_w, x1_all, ys, n, tile, block0):
    row0 = block0 * tile
    steps = n // tile
    pos3 = pos_flat[row0 * TOP_K:(row0 + n) * TOP_K].reshape(steps, 1, tile * TOP_K)
    gw = gate_w[row0:row0 + n]
    return pl.pallas_call(
        _combine_body,
        out_shape=jax.ShapeDtypeStruct((n, D_MODEL), F32),
        grid=(steps,),
        in_specs=[pl.BlockSpec((1, 1, tile * TOP_K), lambda i: (i, 0, 0),
                               memory_space=pltpu.SMEM),
                  pl.BlockSpec((tile, TOP_K), lambda i: (i, 0)),
                  pl.BlockSpec((tile, D_MODEL), lambda i: (block0 + i, 0)),
                  pl.BlockSpec(memory_space=pl.ANY)],
        out_specs=pl.BlockSpec((tile, D_MODEL), lambda i: (i, 0)),
        scratch_shapes=[pltpu.VMEM((TOP_K, tile, D_MODEL), F32), pltpu.SemaphoreType.DMA((1,))],
        compiler_params=_cparams(("arbitrary",), 40),
        name="moe_combine",
    )(pos3, gw, x1_all, ys)
```

```python
import functools
import math

import jax
import jax.numpy as jnp
from jax import lax
from jax.experimental import pallas as pl
from jax.experimental.pallas import tpu as pltpu

F32, BF16, I32 = jnp.float32, jnp.bfloat16, jnp.int32

D_MODEL = 2048
C_CONV = 1024
CONV_W = 31
N_HEADS = 8
DIFF_HD = 64
HEAD_W = 2 * DIFF_HD
QKV_W = N_HEADS * HEAD_W
N_EXPERTS = 32
TOP_K = 4
D_FF = 2048
SWIGLU_LIMIT = 7.0
SWIGLU_ALPHA = 1.702
RMS_EPS = 1e-6
LN_EPS = 1e-5
LAM_INIT = 0.8 - 0.6 * math.exp(-0.3 * 0)
PAGE = 128
MASK_VALUE = -1e30

LANES = 128
MXU_DIM = 256

ROW_TILE = 512
CONV_TILE = 256
CONV_CHUNK = 64
CONV_LANES = 256
CONV_LN_ROWS = 32
CONV_HALO = 32
ATTN_TILE = 512
DEC_PAGES = 4
MERGE_TILE = 256
MOE_SUB = 256
MOE_TAIL = 128
MOE_CAP = 2304
MOE_FF = 256
DISPATCH_TILE = 384
COMBINE_TILE = 256


def _cparams(semantics, vmem_mb=None):
    kw = dict(dimension_semantics=semantics)
    if vmem_mb is not None:
        kw["vmem_limit_bytes"] = vmem_mb * 1024 * 1024
    return pltpu.CompilerParams(**kw)


def _sigmoid(x):
    return 1.0 / (1.0 + jnp.exp(-x))


def _dot(a, b):
    return jnp.dot(a, b, preferred_element_type=F32)


def _rms(x, g):
    ms = jnp.mean(x * x, axis=-1, keepdims=True)
    return x * lax.rsqrt(ms + RMS_EPS) * g


def _dot3(a, w_hi, w_lo):
    a_hi = a.astype(BF16)
    a_lo = (a - a_hi.astype(F32)).astype(BF16)
    return _dot(a_hi, w_hi) + _dot(a_lo, w_hi) + _dot(a_hi, w_lo)


def _split_bf16(w):
    top = lax.reduce_precision(w, exponent_bits=8, mantissa_bits=7)
    return top.astype(BF16), (w - top).astype(BF16)


def _rms_body(x_ref, g_ref, o_ref):
    o_ref[...] = _rms(x_ref[...], g_ref[...]).astype(o_ref.dtype)


def _rms_cast(x, g):
    n = x.shape[0]
    return pl.pallas_call(
        _rms_body,
        out_shape=jax.ShapeDtypeStruct((n, D_MODEL), BF16),
        grid=(n // ROW_TILE,),
        in_specs=[pl.BlockSpec((ROW_TILE, D_MODEL), lambda i: (i, 0)),
                  pl.BlockSpec((1, D_MODEL), lambda i: (0, 0))],
        out_specs=pl.BlockSpec((ROW_TILE, D_MODEL), lambda i: (i, 0)),
        compiler_params=_cparams(("arbitrary",)),
        name="rms_in",
    )(x, g)


IN_COLS = 2 * C_CONV + 3 * QKV_W + 2 * D_MODEL


def _sample_inproj_body(x_ref, g_ref, wh_ref, wl_ref, z_ref):
    z_ref[...] = _dot3(_rms(x_ref[...], g_ref[...]), wh_ref[...], wl_ref[...])


def _sample_inproj(x, g, w_hi, w_lo):
    n = x.shape[0]
    wspec = pl.BlockSpec((D_MODEL, QKV_W), lambda j: (0, j))
    return pl.pallas_call(
        _sample_inproj_body,
        out_shape=jax.ShapeDtypeStruct((n, IN_COLS), F32),
        grid=(IN_COLS // QKV_W,),
        in_specs=[pl.BlockSpec((n, D_MODEL), lambda j: (0, 0)),
                  pl.BlockSpec((1, D_MODEL), lambda j: (0, 0)), wspec, wspec],
        out_specs=pl.BlockSpec((n, QKV_W), lambda j: (0, j)),
        compiler_params=_cparams(("arbitrary",)),
        name="sample_inproj",
    )(x, g, w_hi, w_lo)


def _sample_post_body(z_ref, gq_ref, gk_ref, gm_ref, u_ref, q_ref, k_ref, v_ref, sig_ref):
    o1 = 2 * C_CONV
    o2, o3, o4 = o1 + QKV_W, o1 + 2 * QKV_W, o1 + 3 * QKV_W
    gmat = gm_ref[...]
    u_ref[...] = z_ref[:, 0:C_CONV] * _sigmoid(z_ref[:, C_CONV:o1])
    q = _group_rms(z_ref[:, o1:o2], gq_ref[...], gmat)
    q_ref[...] = (q * (DIFF_HD ** -0.5)).astype(BF16)
    k_ref[...] = _group_rms(z_ref[:, o2:o3], gk_ref[...], gmat)
    v_ref[...] = z_ref[:, o3:o4]
    sig_ref[...] = _sigmoid(z_ref[:, o4:])


def _sample_post(z, gq, gk, gmat):
    n = z.shape[0]
    full = lambda a: pl.BlockSpec(a.shape, lambda: (0,) * a.ndim)
    sds = lambda w, dt: jax.ShapeDtypeStruct((n, w), dt)
    outs = (sds(C_CONV, F32), sds(QKV_W, BF16), sds(QKV_W, F32), sds(QKV_W, F32),
            sds(2 * D_MODEL, F32))
    return pl.pallas_call(
        _sample_post_body,
        out_shape=outs,
        in_specs=[full(z), full(gq), full(gk), full(gmat)],
        out_specs=tuple(pl.BlockSpec(o.shape, lambda: (0, 0)) for o in outs),
        name="sample_post",
    )(z, gq, gk, gmat)


def _glu_body(h_ref, wa_ref, wg_ref, u_ref):
    h = h_ref[...]
    a = _dot(h, wa_ref[...])
    g = _dot(h, wg_ref[...])
    u_ref[...] = a * _sigmoid(g)


def _glu_proj(h, w_in):
    n = h.shape[0]
    return pl.pallas_call(
        _glu_body,
        out_shape=jax.ShapeDtypeStruct((n, C_CONV), F32),
        grid=(n // ROW_TILE,),
        in_specs=[pl.BlockSpec((ROW_TILE, D_MODEL), lambda i: (i, 0)),
                  pl.BlockSpec((D_MODEL, C_CONV), lambda i: (0, 0)),
                  pl.BlockSpec((D_MODEL, C_CONV), lambda i: (0, 1))],
        out_specs=pl.BlockSpec((ROW_TILE, C_CONV), lambda i: (i, 0)),
        compiler_params=_cparams(("arbitrary",), 48),
        name="glu_proj",
    )(h, w_in, w_in)


def _group_rms(z, g, gmat):
    zz = z * z
    hi = zz.astype(BF16)
    lo = (zz - hi.astype(F32)).astype(BF16)
    cols = []
    for c in range(QKV_W // MXU_DIM):
        sl = slice(c * MXU_DIM, (c + 1) * MXU_DIM)
        cols.append(_dot(hi[:, sl], gmat) + _dot(lo[:, sl], gmat))
    ss = jnp.concatenate(cols, axis=1)
    return z * lax.rsqrt(ss * (1.0 / DIFF_HD) + RMS_EPS) * g


def _qkv_body(h_ref, wq_ref, wk_ref, wv_ref, gq_ref, gk_ref, gm_ref,
              q_ref, kf_ref, kb_ref, vf_ref, vb_ref):
    h = h_ref[...]
    gmat = gm_ref[...]
    q = _group_rms(_dot(h, wq_ref[...]), gq_ref[...], gmat)
    q_ref[...] = (q * (DIFF_HD ** -0.5)).astype(BF16)
    k = _group_rms(_dot(h, wk_ref[...]), gk_ref[...], gmat)
    kf_ref[...] = k
    kb_ref[...] = k.astype(BF16)
    v = _dot(h, wv_ref[...])
    vf_ref[...] = v
    vb_ref[...] = v.astype(BF16)


def _qkv_proj(h, w_in, gq, gk, gmat):
    n = h.shape[0]
    wspec = lambda j: pl.BlockSpec((D_MODEL, QKV_W), lambda i: (0, j))
    row = lambda w: pl.BlockSpec((ROW_TILE, w), lambda i: (i, 0))
    const = lambda r, c: pl.BlockSpec((r, c), lambda i: (0, 0))
    sds = lambda dt: jax.ShapeDtypeStruct((n, QKV_W), dt)
    return pl.pallas_call(
        _qkv_body,
        out_shape=(sds(BF16), sds(F32), sds(BF16), sds(F32), sds(BF16)),
        grid=(n // ROW_TILE,),
        in_specs=[row(D_MODEL), wspec(2), wspec(3), wspec(4),
                  const(1, QKV_W), const(1, QKV_W), const(MXU_DIM, MXU_DIM)],
        out_specs=(row(QKV_W),) * 5,
        compiler_params=_cparams(("arbitrary",), 56),
        name="qkv_proj",
    )(h, w_in, w_in, w_in, gq, gk, gmat)


def _gate_body(h_ref, w_ref, o_ref):
    o_ref[...] = _sigmoid(_dot(h_ref[...], w_ref[...]))


def _gate_proj(h, w_in):
    n = h.shape[0]
    first = (2 * C_CONV + 3 * QKV_W) // QKV_W
    return pl.pallas_call(
        _gate_body,
        out_shape=jax.ShapeDtypeStruct((n, 2 * D_MODEL), F32),
        grid=(2 * D_MODEL // QKV_W, n // ROW_TILE),
        in_specs=[pl.BlockSpec((ROW_TILE, D_MODEL), lambda j, i: (i, 0)),
                  pl.BlockSpec((D_MODEL, QKV_W), lambda j, i: (0, first + j))],
        out_specs=pl.BlockSpec((ROW_TILE, QKV_W), lambda j, i: (i, j)),
        compiler_params=_cparams(("arbitrary", "arbitrary"), 40),
        name="gate_proj",
    )(h, w_in)


def _ln_swish(c, g, b):
    mu = jnp.mean(c, axis=-1, keepdims=True)
    d = c - mu
    var = jnp.mean(d * d, axis=-1, keepdims=True)
    y = d * lax.rsqrt(var + LN_EPS) * g + b
    return y * _sigmoid(y)


def _conv_body(u_ref, w_ref, b_ref, g_ref, beta_ref, o_ref, ext_ref, c_ref):
    i = pl.program_id(1)

    @pl.when(i == 0)
    def _():
        ext_ref[0:CONV_HALO, :] = jnp.zeros((CONV_HALO, C_CONV), F32)

    @pl.when(i > 0)
    def _():
        ext_ref[0:CONV_HALO, :] = ext_ref[CONV_TILE:CONV_TILE + CONV_HALO, :]

    ext_ref[CONV_HALO:, :] = u_ref[...]
    lead = CONV_HALO - (CONV_W - 1)
    for r0 in range(0, CONV_TILE, CONV_CHUNK):
        for c0 in range(0, C_CONV, CONV_LANES):
            cols = slice(c0, c0 + CONV_LANES)
            out = None
            for b in range(8):
                rows = CONV_CHUNK + (8 if b else 0)
                part = None
                for a in range((lead + CONV_W + 7) // 8):
                    jp = 8 * a + b
                    if jp < lead or jp >= lead + CONV_W:
                        continue
                    term = ext_ref[r0 + 8 * a:r0 + 8 * a + rows, cols] * w_ref[jp:jp + 1, cols]
                    part = term if part is None else part + term
                if part is None:
                    continue
                shifted = part[b:b + CONV_CHUNK, :]
                out = shifted if out is None else out + shifted
            c_ref[r0:r0 + CONV_CHUNK, cols] = out + b_ref[:, cols]
    for r0 in range(0, CONV_TILE, CONV_LN_ROWS):
        y = _ln_swish(c_ref[r0:r0 + CONV_LN_ROWS, :], g_ref[...], beta_ref[...])
        o_ref[r0:r0 + CONV_LN_ROWS, :] = y.astype(o_ref.dtype)


def _conv_prompt(u_all, n_batch, seq, dw_w, dw_b, ln_g, ln_b):
    nt = seq // CONV_TILE
    const = lambda r: pl.BlockSpec((r, C_CONV), lambda b, i: (0, 0))
    lead = CONV_HALO - (CONV_W - 1)
    taps = jnp.pad(dw_w[:CONV_W], ((lead, -(lead + CONV_W) % 8), (0, 0)))
    return pl.pallas_call(
        _conv_body,
        out_shape=jax.ShapeDtypeStruct((n_batch * seq, C_CONV), BF16),
        grid=(n_batch, nt),
        in_specs=[pl.BlockSpec((CONV_TILE, C_CONV), lambda b, i: (b * nt + i, 0)),
                  const(taps.shape[0]), const(1), const(1), const(1)],
        out_specs=pl.BlockSpec((CONV_TILE, C_CONV), lambda b, i: (b * nt + i, 0)),
        scratch_shapes=[pltpu.VMEM((CONV_HALO + CONV_TILE, C_CONV), F32),
                        pltpu.VMEM((CONV_TILE, C_CONV), F32)],
        compiler_params=_cparams(("arbitrary", "arbitrary")),
        name="conv_prompt",
    )(u_all, taps, dw_b, ln_g, ln_b)


def _conv_sample_body(ext_ref, w_ref, b_ref, g_ref, beta_ref, o_ref, *, t_new):
    acc = jnp.zeros((t_new, C_CONV), F32) + b_ref[...]
    for j in range(CONV_W):
        acc = acc + ext_ref[0, j:j + t_new, :] * w_ref[j:j + 1, :]
    y = _ln_swish(acc, g_ref[...], beta_ref[...])
    o_ref[0] = y.astype(o_ref.dtype)


def _conv_sample(ext, dw_w, dw_b, ln_g, ln_b):
    nb, rows, _ = ext.shape
    t_new = rows - (CONV_W - 1)
    const = lambda r: pl.BlockSpec((r, C_CONV), lambda b: (0, 0))
    return pl.pallas_call(
        functools.partial(_conv_sample_body, t_new=t_new),
        out_shape=jax.ShapeDtypeStruct((nb, t_new, C_CONV), F32),
        grid=(nb,),
        in_specs=[pl.BlockSpec((1, rows, C_CONV), lambda b: (b, 0, 0)),
                  const(CONV_HALO), const(1), const(1), const(1)],
        out_specs=pl.BlockSpec((1, t_new, C_CONV), lambda b: (b, 0, 0)),
        compiler_params=_cparams(("arbitrary",)),
        name="conv_sample",
    )(ext, dw_w, dw_b, ln_g, ln_b)


def _lambda_value(lam_ref):
    lp = lam_ref[...]
    a = jnp.sum(lp[0:1, :] * lp[1:2, :], axis=-1, keepdims=True)
    b = jnp.sum(lp[2:3, :] * lp[3:4, :], axis=-1, keepdims=True)
    return jnp.exp(a) - jnp.exp(b) + LAM_INIT


def _subln(o, g):
    return _rms(o, g) * (1.0 - LAM_INIT)


def _attn_body(qt_ref, k_ref, vt_ref, lam_ref, g_ref, ot_ref, m_ref, l_ref, acc_ref):
    nq, _, t = qt_ref.shape
    lam = _lambda_value(lam_ref)
    lane = lax.broadcasted_iota(I32, (t, HEAD_W), 1)
    key_row = lax.broadcasted_iota(I32, (t, t), 0)
    query_col = lax.broadcasted_iota(I32, (t, t), 1)
    causal = key_row <= query_col

    for qi in range(nq):
        qt = qt_ref[qi]
        m_ref[...] = jnp.full(m_ref.shape, MASK_VALUE, F32)
        l_ref[...] = jnp.zeros(l_ref.shape, F32)
        acc_ref[...] = jnp.zeros(acc_ref.shape, F32)

        def tile(ki, diagonal, qt=qt):
            k = k_ref[pl.ds(pl.multiple_of(ki * t, t), t), :]
            vt = vt_ref[ki]
            zero = jnp.zeros_like(k)
            for m in range(2):
                in_map = (lane < DIFF_HD) if m == 0 else (lane >= DIFF_HD)
                st = _dot(jnp.where(in_map, k, zero), qt)
                if diagonal:
                    st = jnp.where(causal, st, MASK_VALUE)
                m_prev = m_ref[m]
                m_new = jnp.maximum(m_prev, jnp.max(st, axis=0, keepdims=True))
                alpha = jnp.exp(m_prev - m_new)
                p = jnp.exp(st - m_new)
                l_ref[m] = alpha * l_ref[m] + jnp.sum(p, axis=0, keepdims=True)
                acc_ref[m] = alpha * acc_ref[m] + _dot(vt, p.astype(BF16))
                m_ref[m] = m_new

        if qi > 0:
            def off_diagonal(ki, carry):
                tile(ki, False)
                return carry
            lax.fori_loop(0, qi, off_diagonal, 0)
        tile(qi, True)

        o = acc_ref[0] / l_ref[0] - lam * (acc_ref[1] / l_ref[1])
        ms = jnp.mean(o * o, axis=0, keepdims=True)
        o = o * lax.rsqrt(ms + RMS_EPS) * g_ref[...] * (1.0 - LAM_INIT)
        ot_ref[qi] = o.astype(ot_ref.dtype)


def _attn_prompt(qt, k_all, vt, n_batch, seq, lam_p, subln_col):
    nq = seq // ATTN_TILE
    tspec = pl.BlockSpec((nq, HEAD_W, ATTN_TILE), lambda b, h: (b, h, 0))
    return pl.pallas_call(
        _attn_body,
        out_shape=jax.ShapeDtypeStruct(qt.shape, BF16),
        grid=(n_batch, N_HEADS),
        in_specs=[tspec, pl.BlockSpec((seq, HEAD_W), lambda b, h: (b, h)), tspec,
                  pl.BlockSpec((4, DIFF_HD), lambda b, h: (0, 0)),
                  pl.BlockSpec((HEAD_W, 1), lambda b, h: (0, 0))],
        out_specs=tspec,
        scratch_shapes=[pltpu.VMEM((2, 1, ATTN_TILE), F32),
                        pltpu.VMEM((2, 1, ATTN_TILE), F32),
                        pltpu.VMEM((2, HEAD_W, ATTN_TILE), F32)],
        compiler_params=_cparams(("arbitrary", "arbitrary")),
        name="attn_prompt",
    )(qt, k_all, vt, lam_p, subln_col)


def _to_feature_major(x, n_rows):
    return jnp.transpose(x[:n_rows].reshape(n_rows // ATTN_TILE, ATTN_TILE, x.shape[1]), (0, 2, 1))


def _dec_body(pt_ref, q_ref, *refs, t_new):
    del pt_ref
    k_refs = refs[:DEC_PAGES]
    v_refs = refs[DEC_PAGES:2 * DEC_PAGES]
    knew_ref, vnew_ref, lam_ref, g_ref, o_ref, m_ref, l_ref, acc_ref = refs[2 * DEC_PAGES:]
    p = pl.program_id(1)
    page_rows = PAGE * N_HEADS
    half = N_HEADS * t_new

    @pl.when(p == 0)
    def _():
        m_ref[...] = jnp.full(m_ref.shape, MASK_VALUE, F32)
        l_ref[...] = jnp.zeros(l_ref.shape, F32)
        acc_ref[...] = jnp.zeros(acc_ref.shape, F32)

    qrows = q_ref[0]

    def head_mask(n_keys):
        r = lax.broadcasted_iota(I32, (2 * half, n_keys), 0)
        c = lax.broadcasted_iota(I32, (2 * half, n_keys), 1)
        return r, c, (c % N_HEADS) == ((r % half) // t_new)

    def update(kbs, vbs, keep):
        scores = [jnp.where(keep, lax.dot_general(qrows, kb, (((1,), (1,)), ((), ())),
                                                  preferred_element_type=F32), MASK_VALUE)
                  for kb in kbs]
        m_prev = m_ref[...]
        m_new = m_prev
        for s in scores:
            m_new = jnp.maximum(m_new, jnp.max(s, axis=1, keepdims=True))
        alpha = jnp.exp(m_prev - m_new)
        l_new = alpha * l_ref[...]
        acc = alpha * acc_ref[...]
        for s, vb in zip(scores, vbs):
            pr = jnp.exp(s - m_new)
            l_new = l_new + jnp.sum(pr, axis=1, keepdims=True)
            acc = acc + _dot(pr.astype(BF16), vb)
        l_ref[...] = l_new
        acc_ref[...] = acc
        m_ref[...] = m_new

    _, _, keep_page = head_mask(page_rows)
    update([k_refs[j][0].astype(BF16) for j in range(DEC_PAGES)],
           [v_refs[j][0].astype(BF16) for j in range(DEC_PAGES)], keep_page)

    @pl.when(p == pl.num_programs(1) - 1)
    def _():
        r, c, keep = head_mask(knew_ref.shape[1])
        causal = ((c // N_HEADS) <= (r % t_new)) & (c < half)
        update([knew_ref[0]], [vnew_ref[0]], keep & causal)
        o_maps = acc_ref[...] / l_ref[...]
        lam = _lambda_value(lam_ref)
        o = o_maps[0:half, :] - lam * o_maps[half:2 * half, :]
        o_ref[0] = _subln(o, g_ref[...]).astype(o_ref.dtype)


def _attn_sample(qcols, cache_k, cache_v, page_table, k_new, v_new, lam_p, subln_g, t_new):
    nb = qcols.shape[0]
    n_pages = page_table.shape[1]
    steps = n_pages // DEC_PAGES
    page_rows = PAGE * N_HEADS
    ck = cache_k.reshape(cache_k.shape[0], page_rows, HEAD_W)
    cv = cache_v.reshape(cache_v.shape[0], page_rows, HEAD_W)
    pt = page_table.reshape(-1)

    def page_spec(j):
        return pl.BlockSpec((1, page_rows, HEAD_W),
                            lambda b, p, pt_ref: (pt_ref[b * n_pages + p * DEC_PAGES + j], 0, 0))

    per_b = lambda r, c: pl.BlockSpec((1, r, c), lambda b, p, pt_ref: (b, 0, 0))
    const = lambda r, c: pl.BlockSpec((r, c), lambda b, p, pt_ref: (0, 0))
    rows_new = t_new * N_HEADS
    q_rows = 2 * rows_new
    grid_spec = pltpu.PrefetchScalarGridSpec(
        num_scalar_prefetch=1,
        grid=(nb, steps),
        in_specs=[per_b(q_rows, HEAD_W)]
                 + [page_spec(j) for j in range(DEC_PAGES)] * 2
                 + [per_b(k_new.shape[1], HEAD_W), per_b(v_new.shape[1], HEAD_W),
                    const(4, DIFF_HD), const(1, HEAD_W)],
        out_specs=per_b(rows_new, HEAD_W),
        scratch_shapes=[pltpu.VMEM((q_rows, 1), F32), pltpu.VMEM((q_rows, 1), F32),
                        pltpu.VMEM((q_rows, HEAD_W), F32)],
    )
    return pl.pallas_call(
        functools.partial(_dec_body, t_new=t_new),
        out_shape=jax.ShapeDtypeStruct((nb, rows_new, HEAD_W), F32),
        grid_spec=grid_spec,
        compiler_params=_cparams(("arbitrary", "arbitrary")),
        name="attn_sample",
    )(pt, qcols, *([ck] * DEC_PAGES), *([cv] * DEC_PAGES), k_new, v_new, lam_p, subln_g)


def _route(x1, g2_ref, rwh_ref, rwl_ref, rb_ref, idx_ref, gw_ref):
    tm = x1.shape[0]
    logits = _dot3(_rms(x1, g2_ref[...]), rwh_ref[...], rwl_ref[...]) + rb_ref[...]
    lane = lax.broadcasted_iota(I32, (tm, LANES), 1)
    v = jnp.where(lane < N_EXPERTS, logits, -jnp.inf)
    kcol = lax.broadcasted_iota(I32, (tm, TOP_K), 1)
    idx_out = jnp.zeros((tm, TOP_K), I32)
    val_out = jnp.zeros((tm, TOP_K), F32)
    for k in range(TOP_K):
        mx = jnp.max(v, axis=1, keepdims=True)
        ix = jnp.min(jnp.where(v == mx, lane, LANES), axis=1, keepdims=True)
        idx_out = jnp.where(kcol == k, ix, idx_out)
        val_out = jnp.where(kcol == k, mx, val_out)
        v = jnp.where(lane == ix, -jnp.inf, v)
    e = jnp.exp(val_out - val_out[:, 0:1])
    idx_ref[...] = idx_out
    gw_ref[...] = e / jnp.sum(e, axis=1, keepdims=True)


def _merge_body(x_ref, c_ref, o_ref, sc_ref, sa_ref, wc_ref, wa_ref, wo_ref, g2_ref,
                rwh_ref, rwl_ref, rb_ref, x1_ref, idx_ref, gw_ref):
    c_out = _dot(c_ref[...], wc_ref[...])
    a_out = _dot(o_ref[...], wa_ref[...])
    merged = sc_ref[...] * c_out + sa_ref[...] * a_out
    x1 = x_ref[...] + _dot(merged.astype(BF16), wo_ref[...])
    x1_ref[...] = x1
    _route(x1, g2_ref, rwh_ref, rwl_ref, rb_ref, idx_ref, gw_ref)


def _merge(x, c_act, o_act, sig, wc, wa, wo, g2, rwh, rwl, rb, total_rows):
    n = x.shape[0]
    tile = MERGE_TILE
    row = lambda w: pl.BlockSpec((tile, w), lambda i: (i, 0))
    const = lambda r, c: pl.BlockSpec((r, c), lambda i: (0, 0))
    return pl.pallas_call(
        _merge_body,
        out_shape=(jax.ShapeDtypeStruct((total_rows, D_MODEL), F32),
                   jax.ShapeDtypeStruct((n, TOP_K), I32),
                   jax.ShapeDtypeStruct((n, TOP_K), F32)),
        grid=(n // tile,),
        in_specs=[row(D_MODEL), row(C_CONV), row(QKV_W),
                  pl.BlockSpec((tile, D_MODEL), lambda i: (i, 0)),
                  pl.BlockSpec((tile, D_MODEL), lambda i: (i, 1)),
                  const(C_CONV, D_MODEL), const(QKV_W, D_MODEL), const(D_MODEL, D_MODEL),
                  const(1, D_MODEL), const(D_MODEL, LANES), const(D_MODEL, LANES),
                  const(1, LANES)],
        out_specs=(row(D_MODEL), row(TOP_K), row(TOP_K)),
        compiler_params=_cparams(("arbitrary",), 60),
        name="merge_router",
    )(x, c_act, o_act, sig, sig, wc, wa, wo, g2, rwh, rwl, rb)


SAMPLE_COLS = 512


def _sample_mix_body(c_ref, o_ref, sc_ref, sa_ref, wch_ref, wcl_ref, wah_ref, wal_ref, m_ref):
    m_ref[...] = (sc_ref[...] * _dot3(c_ref[...], wch_ref[...], wcl_ref[...])
                  + sa_ref[...] * _dot3(o_ref[...], wah_ref[...], wal_ref[...]))


def _sample_mix(c_act, o_act, sig, wc, wa):
    n = c_act.shape[0]
    nblk = D_MODEL // SAMPLE_COLS
    full = lambda w: pl.BlockSpec((n, w), lambda j: (0, 0))
    wspec = pl.BlockSpec((C_CONV, SAMPLE_COLS), lambda j: (0, j))
    return pl.pallas_call(
        _sample_mix_body,
        out_shape=jax.ShapeDtypeStruct((n, D_MODEL), F32),
        grid=(nblk,),
        in_specs=[full(C_CONV), full(QKV_W),
                  pl.BlockSpec((n, SAMPLE_COLS), lambda j: (0, j)),
                  pl.BlockSpec((n, SAMPLE_COLS), lambda j: (0, nblk + j)),
                  wspec, wspec, wspec, wspec],
        out_specs=pl.BlockSpec((n, SAMPLE_COLS), lambda j: (0, j)),
        compiler_params=_cparams(("arbitrary",)),
        name="sample_mix",
    )(c_act, o_act, sig, sig, wc[0], wc[1], wa[0], wa[1])


def _sample_out_body(x_ref, m_ref, woh_ref, wol_ref, g2_ref, rwh_ref, rwl_ref, rb_ref, prev_ref,
                     x1_ref, idx_ref, gw_ref, acc_ref):
    del prev_ref
    kb = pl.program_id(0)

    @pl.when(kb == 0)
    def _():
        acc_ref[...] = x_ref[...]

    acc_ref[...] = acc_ref[...] + _dot3(m_ref[...], woh_ref[...], wol_ref[...])

    @pl.when(kb == pl.num_programs(0) - 1)
    def _():
        x1 = acc_ref[...]
        x1_ref[...] = x1
        _route(x1, g2_ref, rwh_ref, rwl_ref, rb_ref, idx_ref, gw_ref)


def _sample_out(x, merged, wo, g2, rwh, rwl, rb, x1_prompt):
    n = x.shape[0]
    total_rows = x1_prompt.shape[0]
    assert (total_rows - n) % n == 0
    const = lambda r, c: pl.BlockSpec((r, c), lambda k: (0, 0))
    wspec = pl.BlockSpec((SAMPLE_COLS, D_MODEL), lambda k: (k, 0))
    return pl.pallas_call(
        _sample_out_body,
        out_shape=(jax.ShapeDtypeStruct((total_rows, D_MODEL), F32),
                   jax.ShapeDtypeStruct((n, TOP_K), I32),
                   jax.ShapeDtypeStruct((n, TOP_K), F32)),
        grid=(D_MODEL // SAMPLE_COLS,),
        in_specs=[const(n, D_MODEL), pl.BlockSpec((n, SAMPLE_COLS), lambda k: (0, k)),
                  wspec, wspec, const(1, D_MODEL), const(D_MODEL, LANES), const(D_MODEL, LANES),
                  const(1, LANES), pl.BlockSpec(memory_space=pl.ANY)],
        out_specs=(pl.BlockSpec((n, D_MODEL), lambda k: ((total_rows - n) // n, 0)),
                   const(n, TOP_K), const(n, TOP_K)),
        scratch_shapes=[pltpu.VMEM((n, D_MODEL), F32)],
        input_output_aliases={8: 0},
        compiler_params=_cparams(("arbitrary",)),
        name="sample_out",
    )(x, merged, wo[0], wo[1], g2, rwh, rwl, rb, x1_prompt)


def _routing_tables(top_idx):
    n = top_idx.shape[0]
    nk = n * TOP_K
    eids = jnp.arange(N_EXPERTS, dtype=I32)
    onehot = (top_idx[:, :, None] == eids[None, None, :]).astype(I32)
    per_tok = jnp.sum(onehot, axis=1)
    csum = jnp.cumsum(per_tok, axis=0)
    counts = csum[-1]
    rank = jnp.take_along_axis(csum - per_tok, top_idx, axis=1)
    padded = (counts + MOE_SUB - 1) // MOE_SUB * MOE_SUB
    gstart = jnp.cumsum(padded) - padded
    pos = (gstart[top_idx] + rank).astype(I32)

    n_sb_max = nk // MOE_CAP + N_EXPERTS
    nsb = (counts + MOE_CAP - 1) // MOE_CAP
    cum = jnp.cumsum(nsb)
    total = cum[-1]
    s = jnp.arange(n_sb_max, dtype=I32)
    ex = jnp.minimum(jnp.sum((cum[None, :] <= s[:, None]).astype(I32), axis=1), N_EXPERTS - 1)
    ex_last = ex[jnp.maximum(total - 1, 0)]
    valid = s < total
    ex = jnp.where(valid, ex, ex_last)
    local = s - (cum[ex] - nsb[ex])
    rows = jnp.where(valid, jnp.clip(counts[ex] - local * MOE_CAP, 0, MOE_CAP), 0).astype(I32)
    start = jnp.where(valid, gstart[ex] + local * MOE_CAP, 0).astype(I32)
    n_rows = (nk + N_EXPERTS * (MOE_SUB - 1) + MOE_SUB - 1) // MOE_SUB * MOE_SUB
    return pos, ex, start, rows, n_rows


def _dispatch_body(pos_ref, x_ref, g_ref, xs_ref, hbuf, sem):
    i = pl.program_id(0)
    n = pl.num_programs(0)
    tb = x_ref.shape[0]
    slot = lax.rem(i, 2)

    def wait_slot(sl):
        pltpu.make_async_copy(xs_ref.at[pl.ds(0, TOP_K * tb)], xs_ref.at[pl.ds(0, TOP_K * tb)],
                              sem.at[sl]).wait()

    @pl.when(i >= 2)
    def _():
        wait_slot(slot)

    hbuf[slot] = _rms(x_ref[...], g_ref[...])

    def body(t, carry):
        for k in range(TOP_K):
            p = pos_ref[0, 0, t * TOP_K + k]
            pltpu.make_async_copy(hbuf.at[slot, pl.ds(t, 1)], xs_ref.at[pl.ds(p, 1)],
                                  sem.at[slot]).start(priority=k % 2)
        return carry

    lax.fori_loop(0, tb, body, 0, unroll=4)

    @pl.when(i == n - 1)
    def _():
        wait_slot(slot)
        wait_slot(1 - slot)


def _dispatch(x1_all, g2, pos, n_rows):
    n = x1_all.shape[0]
    tb = DISPATCH_TILE
    steps = n // tb
    assert steps >= 2
    pos3 = pos.reshape(steps, 1, tb * TOP_K)
    return pl.pallas_call(
        _dispatch_body,
        out_shape=jax.ShapeDtypeStruct((n_rows, D_MODEL), F32),
        grid=(steps,),
        in_specs=[pl.BlockSpec((1, 1, tb * TOP_K), lambda i: (i, 0, 0), memory_space=pltpu.SMEM),
                  pl.BlockSpec((tb, D_MODEL), lambda i: (i, 0)),
                  pl.BlockSpec((1, D_MODEL), lambda i: (0, 0))],
        out_specs=pl.BlockSpec(memory_space=pl.ANY),
        scratch_shapes=[pltpu.VMEM((2, tb, D_MODEL), F32), pltpu.SemaphoreType.DMA((2,))],
        compiler_params=_cparams(("arbitrary",)),
        name="moe_dispatch",
    )(pos3, x1_all, g2)


def _expert_body(sbe_ref, sbs_ref, sbr_ref, xs_ref, wg_ref, wu_ref, wd_ref, bg_ref, bu_ref,
                 bd_ref, ys_ref, x_buf, y_buf, wgu_bf, wd_bf, a_buf, x_stage, sem_in, sem_out):
    del sbe_ref
    s = pl.program_id(0)
    f = pl.program_id(1)
    n_f = pl.num_programs(1)
    rows = sbr_ref[s]
    start = pl.multiple_of(sbs_ref[s], MOE_SUB)
    nsub = lax.shift_right_logical(rows + (MOE_SUB - 1), int(math.log2(MOE_SUB)))

    def sub_rows(j):
        return pl.ds(pl.multiple_of(j * MOE_SUB, MOE_SUB), MOE_SUB)

    def copy_in(j, sl):
        return pltpu.make_async_copy(xs_ref.at[pl.ds(start + j * MOE_SUB, MOE_SUB)],
                                     x_stage.at[sl], sem_in.at[sl])

    def copy_out(j):
        return pltpu.make_async_copy(y_buf.at[sub_rows(j)],
                                     ys_ref.at[pl.ds(start + j * MOE_SUB, MOE_SUB)],
                                     sem_out.at[0])

    @pl.when(rows > 0)
    def _():
        wgu_bf[:, 0:MOE_FF] = wg_ref[0].astype(BF16)
        wgu_bf[:, MOE_FF:] = wu_ref[0].astype(BF16)
        wd_bf[...] = wd_ref[0].astype(BF16)

        def fetch(j):
            sl = lax.rem(j, 2)
            copy_in(j, sl).wait()
            rid = lax.broadcasted_iota(I32, (MOE_SUB, D_MODEL), 0)
            xv = jnp.where(rid < rows - j * MOE_SUB, x_stage[sl], 0.0)
            x_buf[sub_rows(j), :] = xv.astype(BF16)

            @pl.when(j + 2 < nsub)
            def _():
                copy_in(j + 2, sl).start()

            y_buf[sub_rows(j), :] = jnp.broadcast_to(bd_ref[0], (MOE_SUB, D_MODEL))

        def activate(j, slot, nrows=MOE_SUB):
            r = pl.ds(pl.multiple_of(j * MOE_SUB, MOE_SUB), nrows)
            gu = _dot(x_buf[r, :], wgu_bf[...])
            g = jnp.minimum(gu[:, 0:MOE_FF] + bg_ref[0], SWIGLU_LIMIT)
            u = jnp.clip(gu[:, MOE_FF:] + bu_ref[0], -SWIGLU_LIMIT, SWIGLU_LIMIT)
            a = (u + 1.0) * (g * _sigmoid(SWIGLU_ALPHA * g))
            a_buf[slot, 0:nrows, :] = a.astype(BF16)

        def project(j, slot, nrows=MOE_SUB):
            r = pl.ds(pl.multiple_of(j * MOE_SUB, MOE_SUB), nrows)
            y_buf[r, :] = y_buf[r, :] + _dot(a_buf[slot, 0:nrows, :], wd_bf[...])

        rem = rows - (nsub - 1) * MOE_SUB
        has_tail = rem <= MOE_TAIL
        nmain = nsub - has_tail.astype(I32)

        def run(first, last):
            if first:
                copy_in(0, 0).start()

                @pl.when(nsub > 1)
                def _():
                    copy_in(1, 1).start()

            def step(j, slot):
                if first:
                    fetch(j)
                activate(j, slot)
                project(j - 1, 1 - slot)
                if last:
                    copy_out(j - 1).start()

            @pl.when(nmain > 0)
            def _():
                if first:
                    fetch(0)
                activate(0, 0)

                def pair(i, carry):
                    j = 2 * i + 1
                    step(j, 1)
                    step(j + 1, 0)
                    return carry

                lax.fori_loop(0, lax.shift_right_logical(nmain - 1, 1), pair, 0)

                @pl.when(lax.rem(nmain - 1, 2) == 1)
                def _():
                    step(nmain - 1, 1)

                project(nmain - 1, lax.rem(nmain - 1, 2))
                if last:
                    copy_out(nmain - 1).start()

            @pl.when(has_tail)
            def _():
                if first:
                    fetch(nmain)
                activate(nmain, 0, MOE_TAIL)
                project(nmain, 0, MOE_TAIL)
                if last:
                    copy_out(nmain).start()

            if last:
                def drain(j, carry):
                    copy_out(j).wait()
                    return carry

                lax.fori_loop(0, nsub, drain, 0)

        n_f_static = D_FF // MOE_FF

        @pl.when(f == 0)
        def _():
            run(True, n_f_static == 1)

        if n_f_static > 2:
            @pl.when((f > 0) & (f < n_f - 1))
            def _():
                run(False, False)

        if n_f_static > 1:
            @pl.when(f == n_f - 1)
            def _():
                run(False, True)


def _experts(xs, sb_e, sb_start, sb_rows, w_gate_up, b_gate_up, w_down, b_down):
    n_rows = xs.shape[0]
    n_sb = sb_e.shape[0]
    n_f = D_FF // MOE_FF

    def fsel(s, f, sbr):
        return jnp.where(sbr[s] > 0, f, n_f - 1)

    wg = pl.BlockSpec((1, D_MODEL, MOE_FF), lambda s, f, e, st, r: (e[s], 0, fsel(s, f, r)))
    wu = pl.BlockSpec((1, D_MODEL, MOE_FF), lambda s, f, e, st, r: (e[s], 0, n_f + fsel(s, f, r)))
    wd = pl.BlockSpec((1, MOE_FF, D_MODEL), lambda s, f, e, st, r: (e[s], fsel(s, f, r), 0))
    bg = pl.BlockSpec((1, 1, MOE_FF), lambda s, f, e, st, r: (e[s], 0, fsel(s, f, r)))
    bu = pl.BlockSpec((1, 1, MOE_FF), lambda s, f, e, st, r: (e[s], 0, n_f + fsel(s, f, r)))
    bd = pl.BlockSpec((1, 1, D_MODEL), lambda s, f, e, st, r: (e[s], 0, 0))
    grid_spec = pltpu.PrefetchScalarGridSpec(
        num_scalar_prefetch=3,
        grid=(n_sb, n_f),
        in_specs=[pl.BlockSpec(memory_space=pl.ANY), wg, wu, wd, bg, bu, bd],
        out_specs=pl.BlockSpec(memory_space=pl.ANY),
        scratch_shapes=[pltpu.VMEM((MOE_CAP, D_MODEL), BF16),
                        pltpu.VMEM((MOE_CAP, D_MODEL), F32),
                        pltpu.VMEM((D_MODEL, 2 * MOE_FF), BF16),
                        pltpu.VMEM((MOE_FF, D_MODEL), BF16),
                        pltpu.VMEM((2, MOE_SUB, MOE_FF), BF16),
                        pltpu.VMEM((2, MOE_SUB, D_MODEL), F32),
                        pltpu.SemaphoreType.DMA((2,)),
                        pltpu.SemaphoreType.DMA((1,))],
    )
    return pl.pallas_call(
        _expert_body,
        out_shape=jax.ShapeDtypeStruct((n_rows, D_MODEL), F32),
        grid_spec=grid_spec,
        compiler_params=_cparams(("arbitrary", "arbitrary"), 60),
        name="moe_experts",
    )(sb_e, sb_start, sb_rows, xs, w_gate_up, w_gate_up, w_down,
      b_gate_up.reshape(N_EXPERTS, 1, 2 * D_FF), b_gate_up.reshape(N_EXPERTS, 1, 2 * D_FF),
      b_down.reshape(N_EXPERTS, 1, D_MODEL))


def _combine_body(pos_ref, gw_ref, x1_ref, ys_ref, o_ref, ybuf, sem):
    tb = x1_ref.shape[0]

    def body(t, carry):
        for k in range(TOP_K):
            p = pos_ref[0, 0, t * TOP_K + k]
            pltpu.make_async_copy(ys_ref.at[pl.ds(p, 1)], ybuf.at[k, pl.ds(t, 1)],
                                  sem.at[0]).start(priority=k % 2)
        return carry

    lax.fori_loop(0, tb, body, 0, unroll=4)
    pltpu.make_async_copy(ybuf, ybuf, sem.at[0]).wait()
    gw = gw_ref[...]
    y = x1_ref[...]
    for k in range(TOP_K):
        y = y + gw[:, k:k + 1] * ybuf[k]
    o_ref[...] = y


def kernel(x_prompt, x_sample, cache_k, cache_v, state_conv, page_table, norm1_g, w_in, conv_dw_w,
           conv_dw_b, conv_ln_g, conv_ln_b, w_conv_out, q_norm_g, k_norm_g, lambda_q1, lambda_k1,
           lambda_q2, lambda_k2, subln_g, w_attn_out, w_out, norm2_g, router_w, router_b,
           w_gate_up, b_gate_up, w_down, b_down):
    nb, seq, _ = x_prompt.shape
    db, t_new, _ = x_sample.shape
    n_p, n_s = nb * seq, db * t_new
    n_all = n_p + n_s
    assert n_p % ROW_TILE == 0 and n_all % DISPATCH_TILE == 0 and n_p % n_s == 0
    assert w_in.shape[0] == 1

    row = lambda a: a[0].reshape(1, -1).astype(F32)
    w_in_b, w_in_lo = _split_bf16(w_in[0])
    wc_b, wc_lo = _split_bf16(w_conv_out[0])
    wa_b, wa_lo = _split_bf16(w_attn_out[0])
    wo_b, wo_lo = _split_bf16(w_out[0])
    rwh, rwl = _split_bf16(jnp.pad(router_w[0], ((0, 0), (0, LANES - N_EXPERTS))))
    rb = jnp.pad(router_b[0].astype(F32), (0, LANES - N_EXPERTS)).reshape(1, LANES)
    gq = jnp.tile(q_norm_g[0], QKV_W // DIFF_HD).reshape(1, QKV_W)
    gk = jnp.tile(k_norm_g[0], QKV_W // DIFF_HD).reshape(1, QKV_W)
    blk = jnp.arange(MXU_DIM) // DIFF_HD
    gmat = (blk[:, None] == blk[None, :]).astype(BF16)
    lam_p = jnp.stack([lambda_q1[0], lambda_k1[0], lambda_q2[0], lambda_k2[0]]).astype(F32)
    dw_w = jnp.pad(conv_dw_w[0], ((0, CONV_HALO - CONV_W), (0, 0)))

    xp = x_prompt.reshape(n_p, D_MODEL)
    xs_ = x_sample.reshape(n_s, D_MODEL)
    h_p = _rms_cast(xp, row(norm1_g))
    u_p = _glu_proj(h_p, w_in_b)
    q_p, k_p, kb_p, v_p, vb_p = _qkv_proj(h_p, w_in_b, gq, gk, gmat)
    sig_p = _gate_proj(h_p, w_in_b)
    z_s = _sample_inproj(xs_, row(norm1_g), w_in_b, w_in_lo)
    u_s, q_s, k_s, v_s, sig_s = _sample_post(z_s, gq, gk, gmat)

    conv_args = (dw_w, row(conv_dw_b), row(conv_ln_g), row(conv_ln_b))
    c_p = _conv_prompt(u_p, nb, seq, *conv_args)
    ext_s = jnp.concatenate([state_conv[0], u_s.reshape(db, t_new, C_CONV)], axis=1)
    c_s = _conv_sample(ext_s, *conv_args).reshape(n_s, C_CONV)
    conv_prompt_new = u_p.reshape(nb, seq, C_CONV)[:, seq - (CONV_W - 1):][None]
    conv_sample_new = ext_s[:, t_new:][None]

    ot = _attn_prompt(_to_feature_major(q_p, n_p), kb_p, _to_feature_major(vb_p, n_p),
                      nb, seq, lam_p, subln_g[0].reshape(HEAD_W, 1).astype(F32))
    o_p = jnp.transpose(ot, (0, 2, 1)).reshape(n_p, QKV_W)
    q_s = jnp.transpose(q_s.reshape(db, t_new, N_HEADS, HEAD_W), (0, 2, 1, 3))
    q_s = q_s.reshape(db, 1, N_HEADS * t_new, HEAD_W)
    in_map = ((jnp.arange(HEAD_W) // DIFF_HD)[None, :] == jnp.arange(2)[:, None])[None, :, None, :]
    qrows = jnp.where(in_map, q_s, jnp.zeros_like(q_s)).reshape(db, 2 * N_HEADS * t_new, HEAD_W)
    pad_new = ((0, 0), (0, LANES - t_new * N_HEADS), (0, 0))
    k_new = jnp.pad(k_s.astype(BF16).reshape(db, t_new * N_HEADS, HEAD_W), pad_new)
    v_new = jnp.pad(v_s.astype(BF16).reshape(db, t_new * N_HEADS, HEAD_W), pad_new)
    o_s = _attn_sample(qrows, cache_k[0], cache_v[0], page_table, k_new, v_new, lam_p,
                       row(subln_g), t_new)
    o_s = jnp.transpose(o_s.reshape(db, N_HEADS, t_new, HEAD_W), (0, 2, 1, 3)).reshape(n_s, QKV_W)

    route_w = (row(norm2_g), rwh, rwl, rb)
    x1_all, idx_p, gw_p = _merge(xp, c_p, o_p, sig_p, wc_b, wa_b, wo_b, *route_w, n_all)
    merged_s = _sample_mix(c_s, o_s, sig_s, (wc_b, wc_lo), (wa_b, wa_lo))
    x1_all, idx_s, gw_s = _sample_out(xs_, merged_s, (wo_b, wo_lo), *route_w, x1_all)
    top_idx = jnp.concatenate([idx_p, idx_s], axis=0)
    gate_w = jnp.concatenate([gw_p, gw_s], axis=0)

    pos, sb_e, sb_start, sb_rows, n_rows = _routing_tables(top_idx)
    xs_sorted = _dispatch(x1_all, row(norm2_g), pos.reshape(-1), n_rows)
    ys = _experts(xs_sorted, sb_e, sb_start, sb_rows, w_gate_up[0], b_gate_up[0], w_down[0],
                  b_down[0])
    pos_flat = pos.reshape(-1)
    y_p = _combine_call(pos_flat, gate_w, x1_all, ys, n_p, COMBINE_TILE, 0)
    y_s = _combine_call(pos_flat, gate_w, x1_all, ys, n_s, n_s, n_p // n_s)

    kv_shape = lambda n, lead: (1, lead, n // lead, N_HEADS, HEAD_W)
    return (y_p.reshape(nb, seq, D_MODEL), y_s.reshape(db, t_new, D_MODEL),
            k_p.reshape(kv_shape(n_p, nb)), v_p.reshape(kv_shape(n_p, nb)), conv_prompt_new,
            k_s.reshape(kv_shape(n_s, db)), v_s.reshape(kv_shape(n_s, db)), conv_sample_new)


def _combine_call(pos_flat, gate_w, x1_all, ys, n, tile, block0):
    row0 = block0 * tile
    steps = n // tile
    pos3 = pos_flat[row0 * TOP_K:(row0 + n) * TOP_K].reshape(steps, 1, tile * TOP_K)
    gw = gate_w[row0:row0 + n]
    return pl.pallas_call(
        _combine_body,
        out_shape=jax.ShapeDtypeStruct((n, D_MODEL), F32),
        grid=(steps,),
        in_specs=[pl.BlockSpec((1, 1, tile * TOP_K), lambda i: (i, 0, 0),
                               memory_space=pltpu.SMEM),
                  pl.BlockSpec((tile, TOP_K), lambda i: (i, 0)),
                  pl.BlockSpec((tile, D_MODEL), lambda i: (block0 + i, 0)),
                  pl.BlockSpec(memory_space=pl.ANY)],
        out_specs=pl.BlockSpec((tile, D_MODEL), lambda i: (i, 0)),
        scratch_shapes=[pltpu.VMEM((TOP_K, tile, D_MODEL), F32), pltpu.SemaphoreType.DMA((1,))],
        compiler_params=_cparams(("arbitrary",), 40),
        name="moe_combine",
    )(pos3, gw, x1_all, ys)
```

```python
import functools
import math

import jax
import jax.numpy as jnp
from jax import lax
from jax.experimental import pallas as pl
from jax.experimental.pallas import tpu as pltpu

F32, BF16, I32 = jnp.float32, jnp.bfloat16, jnp.int32

D_MODEL = 2048
C_CONV = 1024
CONV_W = 31
N_HEADS = 8
DIFF_HD = 64
HEAD_W = 2 * DIFF_HD
QKV_W = N_HEADS * HEAD_W
N_EXPERTS = 32
TOP_K = 4
D_FF = 2048
SWIGLU_LIMIT = 7.0
SWIGLU_ALPHA = 1.702
RMS_EPS = 1e-6
LN_EPS = 1e-5
LAM_INIT = 0.8 - 0.6 * math.exp(-0.3 * 0)
PAGE = 128
MASK_VALUE = -1e30

LANES = 128
MXU_DIM = 256

ROW_TILE = 512
CONV_TILE = 256
CONV_CHUNK = 64
CONV_LANES = 256
CONV_LN_ROWS = 32
CONV_HALO = 32
ATTN_TILE = 512
ATTN_KTILE = 512
DEC_PAGES = 8
MERGE_TILE = 512
MERGE_HALF = 256
MOE_SUB = 256
MOE_TAIL = 128
MOE_CAP = 2304
MOE_FF = 256
DISPATCH_TILE = 384
COMBINE_TILE = 256


def _cparams(semantics, vmem_mb=None):
    kw = dict(dimension_semantics=semantics)
    if vmem_mb is not None:
        kw["vmem_limit_bytes"] = vmem_mb * 1024 * 1024
    return pltpu.CompilerParams(**kw)


def _sigmoid(x):
    return 1.0 / (1.0 + jnp.exp(-x))


def _dot(a, b):
    return jnp.dot(a, b, preferred_element_type=F32)


def _rms(x, g):
    ms = jnp.mean(x * x, axis=-1, keepdims=True)
    return x * lax.rsqrt(ms + RMS_EPS) * g


def _hi_lo(v):
    hi = v.astype(BF16)
    return hi, (v - hi.astype(F32)).astype(BF16)


def _dot3(a, w):
    a_hi, a_lo = _hi_lo(a)
    w_hi, w_lo = _hi_lo(w)
    return _dot(a_hi, w_hi) + _dot(a_lo, w_hi) + _dot(a_hi, w_lo)


def _rms_body(x_ref, g_ref, o_ref):
    o_ref[...] = _rms(x_ref[...], g_ref[...]).astype(o_ref.dtype)


def _rms_cast(x, g):
    n = x.shape[0]
    return pl.pallas_call(
        _rms_body,
        out_shape=jax.ShapeDtypeStruct((n, D_MODEL), BF16),
        grid=(n // ROW_TILE,),
        in_specs=[pl.BlockSpec((ROW_TILE, D_MODEL), lambda i: (i, 0)),
                  pl.BlockSpec((1, D_MODEL), lambda i: (0, 0))],
        out_specs=pl.BlockSpec((ROW_TILE, D_MODEL), lambda i: (i, 0)),
        compiler_params=_cparams(("arbitrary",)),
        name="rms_in",
    )(x, g)


IN_COLS = 2 * C_CONV + 3 * QKV_W + 2 * D_MODEL


def _sample_inproj_body(x_ref, g_ref, w_ref, z_ref):
    z_ref[...] = _dot3(_rms(x_ref[...], g_ref[...]), w_ref[...])


def _sample_inproj(x, g, w):
    n = x.shape[0]
    return pl.pallas_call(
        _sample_inproj_body,
        out_shape=jax.ShapeDtypeStruct((n, IN_COLS), F32),
        grid=(IN_COLS // SAMPLE_COLS,),
        in_specs=[pl.BlockSpec((n, D_MODEL), lambda j: (0, 0)),
                  pl.BlockSpec((1, D_MODEL), lambda j: (0, 0)),
                  pl.BlockSpec((D_MODEL, SAMPLE_COLS), lambda j: (0, j))],
        out_specs=pl.BlockSpec((n, SAMPLE_COLS), lambda j: (0, j)),
        compiler_params=_cparams(("arbitrary",), 40),
        name="sample_inproj",
    )(x, g, w)


def _sample_post_body(z_ref, gq_ref, gk_ref, gm_ref, u_ref, q_ref, k_ref, v_ref, sig_ref):
    o1 = 2 * C_CONV
    o2, o3, o4 = o1 + QKV_W, o1 + 2 * QKV_W, o1 + 3 * QKV_W
    gmat = gm_ref[...]
    u_ref[...] = z_ref[:, 0:C_CONV] * _sigmoid(z_ref[:, C_CONV:o1])
    q = _group_rms(z_ref[:, o1:o2], gq_ref[...], gmat)
    q_ref[...] = (q * (DIFF_HD ** -0.5)).astype(BF16)
    k_ref[...] = _group_rms(z_ref[:, o2:o3], gk_ref[...], gmat)
    v_ref[...] = z_ref[:, o3:o4]
    sig_ref[...] = _sigmoid(z_ref[:, o4:])


def _sample_post(z, gq, gk, gmat):
    n = z.shape[0]
    full = lambda a: pl.BlockSpec(a.shape, lambda: (0,) * a.ndim)
    sds = lambda w, dt: jax.ShapeDtypeStruct((n, w), dt)
    outs = (sds(C_CONV, F32), sds(QKV_W, BF16), sds(QKV_W, F32), sds(QKV_W, F32),
            sds(2 * D_MODEL, F32))
    return pl.pallas_call(
        _sample_post_body,
        out_shape=outs,
        in_specs=[full(z), full(gq), full(gk), full(gmat)],
        out_specs=tuple(pl.BlockSpec(o.shape, lambda: (0, 0)) for o in outs),
        name="sample_post",
    )(z, gq, gk, gmat)


def _glu_body(h_ref, wa_ref, wg_ref, u_ref):
    h = h_ref[...]
    a = _dot(h, wa_ref[...])
    g = _dot(h, wg_ref[...])
    u_ref[...] = a * _sigmoid(g)


def _glu_proj(h, w_in):
    n = h.shape[0]
    return pl.pallas_call(
        _glu_body,
        out_shape=jax.ShapeDtypeStruct((n, C_CONV), F32),
        grid=(n // ROW_TILE,),
        in_specs=[pl.BlockSpec((ROW_TILE, D_MODEL), lambda i: (i, 0)),
                  pl.BlockSpec((D_MODEL, C_CONV), lambda i: (0, 0)),
                  pl.BlockSpec((D_MODEL, C_CONV), lambda i: (0, 1))],
        out_specs=pl.BlockSpec((ROW_TILE, C_CONV), lambda i: (i, 0)),
        compiler_params=_cparams(("arbitrary",), 48),
        name="glu_proj",
    )(h, w_in, w_in)


def _group_rms(z, g, gmat):
    zz = z * z
    hi = zz.astype(BF16)
    lo = (zz - hi.astype(F32)).astype(BF16)
    cols = []
    for c in range(QKV_W // MXU_DIM):
        sl = slice(c * MXU_DIM, (c + 1) * MXU_DIM)
        cols.append(_dot(hi[:, sl], gmat) + _dot(lo[:, sl], gmat))
    ss = jnp.concatenate(cols, axis=1)
    return z * lax.rsqrt(ss * (1.0 / DIFF_HD) + RMS_EPS) * g


def _qkv_body(h_ref, wq_ref, wk_ref, wv_ref, gq_ref, gk_ref, gm_ref,
              q_ref, kf_ref, kb_ref, vf_ref, vb_ref):
    h = h_ref[...]
    gmat = gm_ref[...]
    q = _group_rms(_dot(h, wq_ref[...]), gq_ref[...], gmat)
    q_ref[...] = (q * (DIFF_HD ** -0.5)).astype(BF16)
    k = _group_rms(_dot(h, wk_ref[...]), gk_ref[...], gmat)
    kf_ref[...] = k
    kb_ref[...] = k.astype(BF16)
    v = _dot(h, wv_ref[...])
    vf_ref[...] = v
    vb_ref[...] = v.astype(BF16)


def _qkv_proj(h, w_in, gq, gk, gmat):
    n = h.shape[0]
    wspec = lambda j: pl.BlockSpec((D_MODEL, QKV_W), lambda i: (0, j))
    row = lambda w: pl.BlockSpec((ROW_TILE, w), lambda i: (i, 0))
    const = lambda r, c: pl.BlockSpec((r, c), lambda i: (0, 0))
    sds = lambda dt: jax.ShapeDtypeStruct((n, QKV_W), dt)
    return pl.pallas_call(
        _qkv_body,
        out_shape=(sds(BF16), sds(F32), sds(BF16), sds(F32), sds(BF16)),
        grid=(n // ROW_TILE,),
        in_specs=[row(D_MODEL), wspec(2), wspec(3), wspec(4),
                  const(1, QKV_W), const(1, QKV_W), const(MXU_DIM, MXU_DIM)],
        out_specs=(row(QKV_W),) * 5,
        compiler_params=_cparams(("arbitrary",), 56),
        name="qkv_proj",
    )(h, w_in, w_in, w_in, gq, gk, gmat)


def _gate_body(h_ref, w_ref, o_ref):
    o_ref[...] = _sigmoid(_dot(h_ref[...], w_ref[...]))


def _gate_proj(h, w_in):
    n = h.shape[0]
    first = (2 * C_CONV + 3 * QKV_W) // QKV_W
    return pl.pallas_call(
        _gate_body,
        out_shape=jax.ShapeDtypeStruct((n, 2 * D_MODEL), F32),
        grid=(2 * D_MODEL // QKV_W, n // ROW_TILE),
        in_specs=[pl.BlockSpec((ROW_TILE, D_MODEL), lambda j, i: (i, 0)),
                  pl.BlockSpec((D_MODEL, QKV_W), lambda j, i: (0, first + j))],
        out_specs=pl.BlockSpec((ROW_TILE, QKV_W), lambda j, i: (i, j)),
        compiler_params=_cparams(("arbitrary", "arbitrary"), 40),
        name="gate_proj",
    )(h, w_in)


def _ln_swish(c, g, b):
    mu = jnp.mean(c, axis=-1, keepdims=True)
    d = c - mu
    var = jnp.mean(d * d, axis=-1, keepdims=True)
    y = d * lax.rsqrt(var + LN_EPS) * g + b
    return y * _sigmoid(y)


def _conv_body(u_ref, w_ref, b_ref, g_ref, beta_ref, o_ref, ext_ref, c_ref):
    i = pl.program_id(1)

    @pl.when(i == 0)
    def _():
        ext_ref[0:CONV_HALO, :] = jnp.zeros((CONV_HALO, C_CONV), F32)

    @pl.when(i > 0)
    def _():
        ext_ref[0:CONV_HALO, :] = ext_ref[CONV_TILE:CONV_TILE + CONV_HALO, :]

    ext_ref[CONV_HALO:, :] = u_ref[...]
    lead = CONV_HALO - (CONV_W - 1)
    for r0 in range(0, CONV_TILE, CONV_CHUNK):
        for c0 in range(0, C_CONV, CONV_LANES):
            cols = slice(c0, c0 + CONV_LANES)
            out = None
            for b in range(8):
                rows = CONV_CHUNK + (8 if b else 0)
                part = None
                for a in range((lead + CONV_W + 7) // 8):
                    jp = 8 * a + b
                    if jp < lead or jp >= lead + CONV_W:
                        continue
                    term = ext_ref[r0 + 8 * a:r0 + 8 * a + rows, cols] * w_ref[jp:jp + 1, cols]
                    part = term if part is None else part + term
                if part is None:
                    continue
                shifted = part[b:b + CONV_CHUNK, :]
                out = shifted if out is None else out + shifted
            c_ref[r0:r0 + CONV_CHUNK, cols] = out + b_ref[:, cols]
    for r0 in range(0, CONV_TILE, CONV_LN_ROWS):
        y = _ln_swish(c_ref[r0:r0 + CONV_LN_ROWS, :], g_ref[...], beta_ref[...])
        o_ref[r0:r0 + CONV_LN_ROWS, :] = y.astype(o_ref.dtype)


def _conv_prompt(u_all, n_batch, seq, dw_w, dw_b, ln_g, ln_b):
    nt = seq // CONV_TILE
    const = lambda r: pl.BlockSpec((r, C_CONV), lambda b, i: (0, 0))
    lead = CONV_HALO - (CONV_W - 1)
    taps = jnp.pad(dw_w[:CONV_W], ((lead, -(lead + CONV_W) % 8), (0, 0)))
    return pl.pallas_call(
        _conv_body,
        out_shape=jax.ShapeDtypeStruct((n_batch * seq, C_CONV), BF16),
        grid=(n_batch, nt),
        in_specs=[pl.BlockSpec((CONV_TILE, C_CONV), lambda b, i: (b * nt + i, 0)),
                  const(taps.shape[0]), const(1), const(1), const(1)],
        out_specs=pl.BlockSpec((CONV_TILE, C_CONV), lambda b, i: (b * nt + i, 0)),
        scratch_shapes=[pltpu.VMEM((CONV_HALO + CONV_TILE, C_CONV), F32),
                        pltpu.VMEM((CONV_TILE, C_CONV), F32)],
        compiler_params=_cparams(("arbitrary", "arbitrary")),
        name="conv_prompt",
    )(u_all, taps, dw_b, ln_g, ln_b)


def _conv_sample_body(ext_ref, w_ref, b_ref, g_ref, beta_ref, o_ref, *, t_new):
    acc = jnp.zeros((t_new, C_CONV), F32) + b_ref[...]
    for j in range(CONV_W):
        acc = acc + ext_ref[0, j:j + t_new, :] * w_ref[j:j + 1, :]
    y = _ln_swish(acc, g_ref[...], beta_ref[...])
    o_ref[0] = y.astype(o_ref.dtype)


def _conv_sample(ext, dw_w, dw_b, ln_g, ln_b):
    nb, rows, _ = ext.shape
    t_new = rows - (CONV_W - 1)
    const = lambda r: pl.BlockSpec((r, C_CONV), lambda b: (0, 0))
    return pl.pallas_call(
        functools.partial(_conv_sample_body, t_new=t_new),
        out_shape=jax.ShapeDtypeStruct((nb, t_new, C_CONV), F32),
        grid=(nb,),
        in_specs=[pl.BlockSpec((1, rows, C_CONV), lambda b: (b, 0, 0)),
                  const(CONV_HALO), const(1), const(1), const(1)],
        out_specs=pl.BlockSpec((1, t_new, C_CONV), lambda b: (b, 0, 0)),
        compiler_params=_cparams(("arbitrary",)),
        name="conv_sample",
    )(ext, dw_w, dw_b, ln_g, ln_b)


def _lambda_value(lam_ref):
    lp = lam_ref[...]
    a = jnp.sum(lp[0:1, :] * lp[1:2, :], axis=-1, keepdims=True)
    b = jnp.sum(lp[2:3, :] * lp[3:4, :], axis=-1, keepdims=True)
    return jnp.exp(a) - jnp.exp(b) + LAM_INIT


def _subln(o, g):
    return _rms(o, g) * (1.0 - LAM_INIT)


def _attn_body(qt_ref, k_ref, vt_ref, lam_ref, g_ref, ot_ref, m_ref, l_ref, acc_ref):
    nq, _, t = qt_ref.shape
    tk = vt_ref.shape[2]
    per_q = t // tk
    lam = _lambda_value(lam_ref)
    lane = lax.broadcasted_iota(I32, (tk, HEAD_W), 1)
    key_row = lax.broadcasted_iota(I32, (tk, t), 0)
    query_col = lax.broadcasted_iota(I32, (tk, t), 1)

    for qi in range(nq):
        qt = qt_ref[qi]
        m_ref[...] = jnp.full(m_ref.shape, MASK_VALUE, F32)
        l_ref[...] = jnp.zeros(l_ref.shape, F32)
        acc_ref[...] = jnp.zeros(acc_ref.shape, F32)

        def tile(ki, key_offset, qt=qt):
            k = k_ref[pl.ds(pl.multiple_of(ki * tk, tk), tk), :]
            vt = vt_ref[ki]
            zero = jnp.zeros_like(k)
            for m in range(2):
                in_map = (lane < DIFF_HD) if m == 0 else (lane >= DIFF_HD)
                st = _dot(jnp.where(in_map, k, zero), qt)
                if key_offset is not None:
                    st = jnp.where(key_row + key_offset <= query_col, st, MASK_VALUE)
                m_prev = m_ref[m]
                m_new = jnp.maximum(m_prev, jnp.max(st, axis=0, keepdims=True))
                alpha = jnp.exp(m_prev - m_new)
                p = jnp.exp(st - m_new)
                l_ref[m] = alpha * l_ref[m] + jnp.sum(p, axis=0, keepdims=True)
                acc_ref[m] = alpha * acc_ref[m] + _dot(vt, p.astype(BF16))
                m_ref[m] = m_new

        if qi > 0:
            def off_diagonal(ki, carry):
                tile(ki, None)
                return carry
            lax.fori_loop(0, qi * per_q, off_diagonal, 0)
        for d in range(per_q):
            tile(qi * per_q + d, d * tk)

        o = acc_ref[0] / l_ref[0] - lam * (acc_ref[1] / l_ref[1])
        ms = jnp.mean(o * o, axis=0, keepdims=True)
        o = o * lax.rsqrt(ms + RMS_EPS) * g_ref[...] * (1.0 - LAM_INIT)
        ot_ref[qi] = o.astype(ot_ref.dtype)


def _attn_prompt(qt, k_all, vt, n_batch, seq, lam_p, subln_col):
    nq = seq // ATTN_TILE
    nk = seq // ATTN_KTILE
    qspec = pl.BlockSpec((nq, HEAD_W, ATTN_TILE), lambda b, h: (b, h, 0))
    return pl.pallas_call(
        _attn_body,
        out_shape=jax.ShapeDtypeStruct(qt.shape, BF16),
        grid=(n_batch, N_HEADS),
        in_specs=[qspec, pl.BlockSpec((seq, HEAD_W), lambda b, h: (b, h)),
                  pl.BlockSpec((nk, HEAD_W, ATTN_KTILE), lambda b, h: (b, h, 0)),
                  pl.BlockSpec((4, DIFF_HD), lambda b, h: (0, 0)),
                  pl.BlockSpec((HEAD_W, 1), lambda b, h: (0, 0))],
        out_specs=qspec,
        scratch_shapes=[pltpu.VMEM((2, 1, ATTN_TILE), F32),
                        pltpu.VMEM((2, 1, ATTN_TILE), F32),
                        pltpu.VMEM((2, HEAD_W, ATTN_TILE), F32)],
        compiler_params=_cparams(("arbitrary", "arbitrary")),
        name="attn_prompt",
    )(qt, k_all, vt, lam_p, subln_col)


def _to_feature_major(x, tile):
    return jnp.transpose(x.reshape(x.shape[0] // tile, tile, x.shape[1]), (0, 2, 1))


def _dec_body(pt_ref, q_ref, *refs, t_new):
    del pt_ref
    k_refs = refs[:DEC_PAGES]
    v_refs = refs[DEC_PAGES:2 * DEC_PAGES]
    knew_ref, vnew_ref, lam_ref, g_ref, o_ref, m_ref, l_ref, acc_ref = refs[2 * DEC_PAGES:]
    p = pl.program_id(1)
    page_rows = PAGE * N_HEADS
    half = N_HEADS * t_new

    @pl.when(p == 0)
    def _():
        m_ref[...] = jnp.full(m_ref.shape, MASK_VALUE, F32)
        l_ref[...] = jnp.zeros(l_ref.shape, F32)
        acc_ref[...] = jnp.zeros(acc_ref.shape, F32)

    qrows = q_ref[0]

    def head_mask(n_keys):
        r = lax.broadcasted_iota(I32, (2 * half, n_keys), 0)
        c = lax.broadcasted_iota(I32, (2 * half, n_keys), 1)
        return r, c, (c % N_HEADS) == ((r % half) // t_new)

    def update(kbs, vbs, keep):
        scores = [jnp.where(keep, lax.dot_general(qrows, kb, (((1,), (1,)), ((), ())),
                                                  preferred_element_type=F32), MASK_VALUE)
                  for kb in kbs]
        m_prev = m_ref[...]
        m_new = m_prev
        for s in scores:
            m_new = jnp.maximum(m_new, jnp.max(s, axis=1, keepdims=True))
        alpha = jnp.exp(m_prev - m_new)
        l_new = alpha * l_ref[...]
        acc = alpha * acc_ref[...]
        for s, vb in zip(scores, vbs):
            pr = jnp.exp(s - m_new)
            l_new = l_new + jnp.sum(pr, axis=1, keepdims=True)
            acc = acc + _dot(pr.astype(BF16), vb)
        l_ref[...] = l_new
        acc_ref[...] = acc
        m_ref[...] = m_new

    _, _, keep_page = head_mask(page_rows)
    update([k_refs[j][0].astype(BF16) for j in range(DEC_PAGES)],
           [v_refs[j][0].astype(BF16) for j in range(DEC_PAGES)], keep_page)

    @pl.when(p == pl.num_programs(1) - 1)
    def _():
        r, c, keep = head_mask(knew_ref.shape[1])
        causal = ((c // N_HEADS) <= (r % t_new)) & (c < half)
        update([knew_ref[0]], [vnew_ref[0]], keep & causal)
        o_maps = acc_ref[...] / l_ref[...]
        lam = _lambda_value(lam_ref)
        o = o_maps[0:half, :] - lam * o_maps[half:2 * half, :]
        o_ref[0] = _subln(o, g_ref[...]).astype(o_ref.dtype)


def _attn_sample(qcols, cache_k, cache_v, page_table, k_new, v_new, lam_p, subln_g, t_new):
    nb = qcols.shape[0]
    n_pages = page_table.shape[1]
    steps = n_pages // DEC_PAGES
    page_rows = PAGE * N_HEADS
    ck = cache_k.reshape(cache_k.shape[0], page_rows, HEAD_W)
    cv = cache_v.reshape(cache_v.shape[0], page_rows, HEAD_W)
    pt = page_table.reshape(-1)

    def page_spec(j):
        return pl.BlockSpec((1, page_rows, HEAD_W),
                            lambda b, p, pt_ref: (pt_ref[b * n_pages + p * DEC_PAGES + j], 0, 0))

    per_b = lambda r, c: pl.BlockSpec((1, r, c), lambda b, p, pt_ref: (b, 0, 0))
    const = lambda r, c: pl.BlockSpec((r, c), lambda b, p, pt_ref: (0, 0))
    rows_new = t_new * N_HEADS
    q_rows = 2 * rows_new
    grid_spec = pltpu.PrefetchScalarGridSpec(
        num_scalar_prefetch=1,
        grid=(nb, steps),
        in_specs=[per_b(q_rows, HEAD_W)]
                 + [page_spec(j) for j in range(DEC_PAGES)] * 2
                 + [per_b(k_new.shape[1], HEAD_W), per_b(v_new.shape[1], HEAD_W),
                    const(4, DIFF_HD), const(1, HEAD_W)],
        out_specs=per_b(rows_new, HEAD_W),
        scratch_shapes=[pltpu.VMEM((q_rows, 1), F32), pltpu.VMEM((q_rows, 1), F32),
                        pltpu.VMEM((q_rows, HEAD_W), F32)],
    )
    return pl.pallas_call(
        functools.partial(_dec_body, t_new=t_new),
        out_shape=jax.ShapeDtypeStruct((nb, rows_new, HEAD_W), F32),
        grid_spec=grid_spec,
        compiler_params=_cparams(("arbitrary", "arbitrary")),
        name="attn_sample",
    )(pt, qcols, *([ck] * DEC_PAGES), *([cv] * DEC_PAGES), k_new, v_new, lam_p, subln_g)


def _route(x1, g2_ref, rw_ref, rb_ref, idx_ref, gw_ref):
    tm = x1.shape[0]
    logits = _dot3(_rms(x1, g2_ref[...]), rw_ref[...]) + rb_ref[...]
    lane = lax.broadcasted_iota(I32, (tm, LANES), 1)
    v = jnp.where(lane < N_EXPERTS, logits, -jnp.inf)
    kcol = lax.broadcasted_iota(I32, (tm, TOP_K), 1)
    idx_out = jnp.zeros((tm, TOP_K), I32)
    val_out = jnp.zeros((tm, TOP_K), F32)
    for k in range(TOP_K):
        mx = jnp.max(v, axis=1, keepdims=True)
        ix = jnp.min(jnp.where(v == mx, lane, LANES), axis=1, keepdims=True)
        idx_out = jnp.where(kcol == k, ix, idx_out)
        val_out = jnp.where(kcol == k, mx, val_out)
        v = jnp.where(lane == ix, -jnp.inf, v)
    e = jnp.exp(val_out - val_out[:, 0:1])
    idx_ref[...] = idx_out
    gw_ref[...] = e / jnp.sum(e, axis=1, keepdims=True)


def _merge_body(x_ref, c_ref, o_ref, sc_ref, sa_ref, wc_ref, wa_ref, wo_ref, g2_ref,
                rw_ref, rb_ref, x1_ref, idx_ref, gw_ref):
    for r0 in range(0, MERGE_TILE, MERGE_HALF):
        rows = slice(r0, r0 + MERGE_HALF)
        c_out = _dot(c_ref[rows, :], wc_ref[...])
        a_out = _dot(o_ref[rows, :], wa_ref[...])
        merged = sc_ref[rows, :] * c_out + sa_ref[rows, :] * a_out
        x1 = x_ref[rows, :] + _dot(merged.astype(BF16), wo_ref[...])
        x1_ref[rows, :] = x1
        _route(x1, g2_ref, rw_ref, rb_ref, idx_ref.at[rows], gw_ref.at[rows])


def _merge(x, c_act, o_act, sig, wc, wa, wo, g2, rw, rb, total_rows):
    n = x.shape[0]
    tile = MERGE_TILE
    row = lambda w: pl.BlockSpec((tile, w), lambda i: (i, 0))
    const = lambda r, c: pl.BlockSpec((r, c), lambda i: (0, 0), pipeline_mode=pl.Buffered(1))
    return pl.pallas_call(
        _merge_body,
        out_shape=(jax.ShapeDtypeStruct((total_rows, D_MODEL), F32),
                   jax.ShapeDtypeStruct((n, TOP_K), I32),
                   jax.ShapeDtypeStruct((n, TOP_K), F32)),
        grid=(n // tile,),
        in_specs=[row(D_MODEL), row(C_CONV), row(QKV_W),
                  pl.BlockSpec((tile, D_MODEL), lambda i: (i, 0)),
                  pl.BlockSpec((tile, D_MODEL), lambda i: (i, 1)),
                  const(C_CONV, D_MODEL), const(QKV_W, D_MODEL), const(D_MODEL, D_MODEL),
                  const(1, D_MODEL), const(D_MODEL, LANES), const(1, LANES)],
        out_specs=(row(D_MODEL), row(TOP_K), row(TOP_K)),
        compiler_params=_cparams(("arbitrary",), 60),
        name="merge_router",
    )(x, c_act, o_act, sig, sig, wc, wa, wo, g2, rw, rb)


SAMPLE_COLS = 512


def _sample_mix_body(c_ref, o_ref, sc_ref, sa_ref, wc_ref, wa_ref, m_ref):
    m_ref[...] = (sc_ref[...] * _dot3(c_ref[...], wc_ref[...])
                  + sa_ref[...] * _dot3(o_ref[...], wa_ref[...]))


def _sample_mix(c_act, o_act, sig, wc, wa):
    n = c_act.shape[0]
    nblk = D_MODEL // SAMPLE_COLS
    full = lambda w: pl.BlockSpec((n, w), lambda j: (0, 0))
    wspec = pl.BlockSpec((C_CONV, SAMPLE_COLS), lambda j: (0, j))
    return pl.pallas_call(
        _sample_mix_body,
        out_shape=jax.ShapeDtypeStruct((n, D_MODEL), F32),
        grid=(nblk,),
        in_specs=[full(C_CONV), full(QKV_W),
                  pl.BlockSpec((n, SAMPLE_COLS), lambda j: (0, j)),
                  pl.BlockSpec((n, SAMPLE_COLS), lambda j: (0, nblk + j)),
                  wspec, wspec],
        out_specs=pl.BlockSpec((n, SAMPLE_COLS), lambda j: (0, j)),
        compiler_params=_cparams(("arbitrary",)),
        name="sample_mix",
    )(c_act, o_act, sig, sig, wc, wa)


def _sample_out_body(x_ref, m_ref, wo_ref, g2_ref, rw_ref, rb_ref, prev_ref,
                     x1_ref, idx_ref, gw_ref, acc_ref):
    del prev_ref
    kb = pl.program_id(0)

    @pl.when(kb == 0)
    def _():
        acc_ref[...] = x_ref[...]

    acc_ref[...] = acc_ref[...] + _dot3(m_ref[...], wo_ref[...])

    @pl.when(kb == pl.num_programs(0) - 1)
    def _():
        x1 = acc_ref[...]
        x1_ref[...] = x1
        _route(x1, g2_ref, rw_ref, rb_ref, idx_ref, gw_ref)


def _sample_out(x, merged, wo, g2, rw, rb, x1_prompt):
    n = x.shape[0]
    total_rows = x1_prompt.shape[0]
    assert (total_rows - n) % n == 0
    const = lambda r, c: pl.BlockSpec((r, c), lambda k: (0, 0))
    wspec = pl.BlockSpec((SAMPLE_COLS, D_MODEL), lambda k: (k, 0))
    return pl.pallas_call(
        _sample_out_body,
        out_shape=(jax.ShapeDtypeStruct((total_rows, D_MODEL), F32),
                   jax.ShapeDtypeStruct((n, TOP_K), I32),
                   jax.ShapeDtypeStruct((n, TOP_K), F32)),
        grid=(D_MODEL // SAMPLE_COLS,),
        in_specs=[const(n, D_MODEL), pl.BlockSpec((n, SAMPLE_COLS), lambda k: (0, k)),
                  wspec, const(1, D_MODEL), const(D_MODEL, LANES), const(1, LANES),
                  pl.BlockSpec(memory_space=pl.ANY)],
        out_specs=(pl.BlockSpec((n, D_MODEL), lambda k: ((total_rows - n) // n, 0)),
                   const(n, TOP_K), const(n, TOP_K)),
        scratch_shapes=[pltpu.VMEM((n, D_MODEL), F32)],
        input_output_aliases={6: 0},
        compiler_params=_cparams(("arbitrary",)),
        name="sample_out",
    )(x, merged, wo, g2, rw, rb, x1_prompt)


def _routing_tables(top_idx):
    n = top_idx.shape[0]
    nk = n * TOP_K
    eids = jnp.arange(N_EXPERTS, dtype=I32)
    onehot = (top_idx[:, :, None] == eids[None, None, :]).astype(I32)
    per_tok = jnp.sum(onehot, axis=1)
    csum = jnp.cumsum(per_tok, axis=0)
    counts = csum[-1]
    rank = jnp.take_along_axis(csum - per_tok, top_idx, axis=1)
    padded = (counts + MOE_SUB - 1) // MOE_SUB * MOE_SUB
    gstart = jnp.cumsum(padded) - padded
    pos = (gstart[top_idx] + rank).astype(I32)

    n_sb_max = nk // MOE_CAP + N_EXPERTS
    nsb = (counts + MOE_CAP - 1) // MOE_CAP
    cum = jnp.cumsum(nsb)
    total = cum[-1]
    s = jnp.arange(n_sb_max, dtype=I32)
    ex = jnp.minimum(jnp.sum((cum[None, :] <= s[:, None]).astype(I32), axis=1), N_EXPERTS - 1)
    ex_last = ex[jnp.maximum(total - 1, 0)]
    valid = s < total
    ex = jnp.where(valid, ex, ex_last)
    local = s - (cum[ex] - nsb[ex])
    rows = jnp.where(valid, jnp.clip(counts[ex] - local * MOE_CAP, 0, MOE_CAP), 0).astype(I32)
    start = jnp.where(valid, gstart[ex] + local * MOE_CAP, 0).astype(I32)
    n_rows = (nk + N_EXPERTS * (MOE_SUB - 1) + MOE_SUB - 1) // MOE_SUB * MOE_SUB
    return pos, ex, start, rows, n_rows


def _dispatch_body(pos_ref, x_ref, g_ref, xs_ref, hbuf, sem):
    i = pl.program_id(0)
    n = pl.num_programs(0)
    tb = x_ref.shape[0]
    slot = lax.rem(i, 2)

    def wait_slot(sl):
        pltpu.make_async_copy(xs_ref.at[pl.ds(0, TOP_K * tb)], xs_ref.at[pl.ds(0, TOP_K * tb)],
                              sem.at[sl]).wait()

    @pl.when(i >= 2)
    def _():
        wait_slot(slot)

    hbuf[slot] = _rms(x_ref[...], g_ref[...])

    def body(t, carry):
        for k in range(TOP_K):
            p = pos_ref[0, 0, t * TOP_K + k]
            pltpu.make_async_copy(hbuf.at[slot, pl.ds(t, 1)], xs_ref.at[pl.ds(p, 1)],
                                  sem.at[slot]).start(priority=k % 2)
        return carry

    lax.fori_loop(0, tb, body, 0, unroll=4)

    @pl.when(i == n - 1)
    def _():
        wait_slot(slot)
        wait_slot(1 - slot)


def _dispatch(x1_all, g2, pos, n_rows):
    n = x1_all.shape[0]
    tb = DISPATCH_TILE
    steps = n // tb
    assert steps >= 2
    pos3 = pos.reshape(steps, 1, tb * TOP_K)
    return pl.pallas_call(
        _dispatch_body,
        out_shape=jax.ShapeDtypeStruct((n_rows, D_MODEL), F32),
        grid=(steps,),
        in_specs=[pl.BlockSpec((1, 1, tb * TOP_K), lambda i: (i, 0, 0), memory_space=pltpu.SMEM),
                  pl.BlockSpec((tb, D_MODEL), lambda i: (i, 0)),
                  pl.BlockSpec((1, D_MODEL), lambda i: (0, 0))],
        out_specs=pl.BlockSpec(memory_space=pl.ANY),
        scratch_shapes=[pltpu.VMEM((2, tb, D_MODEL), F32), pltpu.SemaphoreType.DMA((2,))],
        compiler_params=_cparams(("arbitrary",)),
        name="moe_dispatch",
    )(pos3, x1_all, g2)


def _expert_body(sbe_ref, sbs_ref, sbr_ref, xs_ref, wg_ref, wu_ref, wd_ref, bg_ref, bu_ref,
                 bd_ref, ys_ref, x_buf, y_buf, wgu_bf, wd_bf, a_buf, x_stage, sem_in, sem_out):
    del sbe_ref
    s = pl.program_id(0)
    f = pl.program_id(1)
    n_f = pl.num_programs(1)
    rows = sbr_ref[s]
    start = pl.multiple_of(sbs_ref[s], MOE_SUB)
    nsub = lax.shift_right_logical(rows + (MOE_SUB - 1), int(math.log2(MOE_SUB)))

    def sub_rows(j):
        return pl.ds(pl.multiple_of(j * MOE_SUB, MOE_SUB), MOE_SUB)

    def copy_in(j, sl):
        return pltpu.make_async_copy(xs_ref.at[pl.ds(start + j * MOE_SUB, MOE_SUB)],
                                     x_stage.at[sl], sem_in.at[sl])

    def copy_out(j):
        return pltpu.make_async_copy(y_buf.at[sub_rows(j)],
                                     ys_ref.at[pl.ds(start + j * MOE_SUB, MOE_SUB)],
                                     sem_out.at[0])

    @pl.when(rows > 0)
    def _():
        wgu_bf[:, 0:MOE_FF] = wg_ref[0].astype(BF16)
        wgu_bf[:, MOE_FF:] = wu_ref[0].astype(BF16)
        wd_bf[...] = wd_ref[0].astype(BF16)

        def fetch(j):
            sl = lax.rem(j, 2)
            copy_in(j, sl).wait()
            rid = lax.broadcasted_iota(I32, (MOE_SUB, D_MODEL), 0)
            xv = jnp.where(rid < rows - j * MOE_SUB, x_stage[sl], 0.0)
            x_buf[sub_rows(j), :] = xv.astype(BF16)

            @pl.when(j + 2 < nsub)
            def _():
                copy_in(j + 2, sl).start()

            y_buf[sub_rows(j), :] = jnp.broadcast_to(bd_ref[0], (MOE_SUB, D_MODEL))

        def activate(j, slot, nrows=MOE_SUB):
            r = pl.ds(pl.multiple_of(j * MOE_SUB, MOE_SUB), nrows)
            gu = _dot(x_buf[r, :], wgu_bf[...])
            g = jnp.minimum(gu[:, 0:MOE_FF] + bg_ref[0], SWIGLU_LIMIT)
            u = jnp.clip(gu[:, MOE_FF:] + bu_ref[0], -SWIGLU_LIMIT, SWIGLU_LIMIT)
            a = (u + 1.0) * (g * _sigmoid(SWIGLU_ALPHA * g))
            a_buf[slot, 0:nrows, :] = a.astype(BF16)

        def project(j, slot, nrows=MOE_SUB):
            r = pl.ds(pl.multiple_of(j * MOE_SUB, MOE_SUB), nrows)
            y_buf[r, :] = y_buf[r, :] + _dot(a_buf[slot, 0:nrows, :], wd_bf[...])

        rem = rows - (nsub - 1) * MOE_SUB
        has_tail = rem <= MOE_TAIL
        nmain = nsub - has_tail.astype(I32)

        def run(first, last):
            if first:
                copy_in(0, 0).start()

                @pl.when(nsub > 1)
                def _():
                    copy_in(1, 1).start()

            def step(j, slot):
                if first:
                    fetch(j)
                activate(j, slot)
                project(j - 1, 1 - slot)
                if last:
                    copy_out(j - 1).start()

            @pl.when(nmain > 0)
            def _():
                if first:
                    fetch(0)
                activate(0, 0)

                def pair(i, carry):
                    j = 2 * i + 1
                    step(j, 1)
                    step(j + 1, 0)
                    return carry

                lax.fori_loop(0, lax.shift_right_logical(nmain - 1, 1), pair, 0)

                @pl.when(lax.rem(nmain - 1, 2) == 1)
                def _():
                    step(nmain - 1, 1)

                project(nmain - 1, lax.rem(nmain - 1, 2))
                if last:
                    copy_out(nmain - 1).start()

            @pl.when(has_tail)
            def _():
                if first:
                    fetch(nmain)
                activate(nmain, 0, MOE_TAIL)
                project(nmain, 0, MOE_TAIL)
                if last:
                    copy_out(nmain).start()

            if last:
                def drain(j, carry):
                    copy_out(j).wait()
                    return carry

                lax.fori_loop(0, nsub, drain, 0)

        n_f_static = D_FF // MOE_FF

        @pl.when(f == 0)
        def _():
            run(True, n_f_static == 1)

        if n_f_static > 2:
            @pl.when((f > 0) & (f < n_f - 1))
            def _():
                run(False, False)

        if n_f_static > 1:
            @pl.when(f == n_f - 1)
            def _():
                run(False, True)


def _experts(xs, sb_e, sb_start, sb_rows, w_gate_up, b_gate_up, w_down, b_down):
    n_rows = xs.shape[0]
    n_sb = sb_e.shape[0]
    n_f = D_FF // MOE_FF

    def fsel(s, f, sbr):
        return jnp.where(sbr[s] > 0, f, n_f - 1)

    wg = pl.BlockSpec((1, D_MODEL, MOE_FF), lambda s, f, e, st, r: (e[s], 0, fsel(s, f, r)))
    wu = pl.BlockSpec((1, D_MODEL, MOE_FF), lambda s, f, e, st, r: (e[s], 0, n_f + fsel(s, f, r)))
    wd = pl.BlockSpec((1, MOE_FF, D_MODEL), lambda s, f, e, st, r: (e[s], fsel(s, f, r), 0))
    bg = pl.BlockSpec((1, 1, MOE_FF), lambda s, f, e, st, r: (e[s], 0, fsel(s, f, r)))
    bu = pl.BlockSpec((1, 1, MOE_FF), lambda s, f, e, st, r: (e[s], 0, n_f + fsel(s, f, r)))
    bd = pl.BlockSpec((1, 1, D_MODEL), lambda s, f, e, st, r: (e[s], 0, 0))
    grid_spec = pltpu.PrefetchScalarGridSpec(
        num_scalar_prefetch=3,
        grid=(n_sb, n_f),
        in_specs=[pl.BlockSpec(memory_space=pl.ANY), wg, wu, wd, bg, bu, bd],
        out_specs=pl.BlockSpec(memory_space=pl.ANY),
        scratch_shapes=[pltpu.VMEM((MOE_CAP, D_MODEL), BF16),
                        pltpu.VMEM((MOE_CAP, D_MODEL), F32),
                        pltpu.VMEM((D_MODEL, 2 * MOE_FF), BF16),
                        pltpu.VMEM((MOE_FF, D_MODEL), BF16),
                        pltpu.VMEM((2, MOE_SUB, MOE_FF), BF16),
                        pltpu.VMEM((2, MOE_SUB, D_MODEL), F32),
                        pltpu.SemaphoreType.DMA((2,)),
                        pltpu.SemaphoreType.DMA((1,))],
    )
    return pl.pallas_call(
        _expert_body,
        out_shape=jax.ShapeDtypeStruct((n_rows, D_MODEL), F32),
        grid_spec=grid_spec,
        compiler_params=_cparams(("arbitrary", "arbitrary"), 60),
        name="moe_experts",
    )(sb_e, sb_start, sb_rows, xs, w_gate_up, w_gate_up, w_down,
      b_gate_up.reshape(N_EXPERTS, 1, 2 * D_FF), b_gate_up.reshape(N_EXPERTS, 1, 2 * D_FF),
      b_down.reshape(N_EXPERTS, 1, D_MODEL))


def _combine_body(pos_ref, gw_ref, x1_ref, ys_ref, o_ref, ybuf, sem):
    tb = x1_ref.shape[0]

    def body(t, carry):
        for k in range(TOP_K):
            p = pos_ref[0, 0, t * TOP_K + k]
            pltpu.make_async_copy(ys_ref.at[pl.ds(p, 1)], ybuf.at[k, pl.ds(t, 1)],
                                  sem.at[0]).start(priority=k % 2)
        return carry

    lax.fori_loop(0, tb, body, 0, unroll=4)
    pltpu.make_async_copy(ybuf, ybuf, sem.at[0]).wait()
    gw = gw_ref[...]
    y = x1_ref[...]
    for k in range(TOP_K):
        y = y + gw[:, k:k + 1] * ybuf[k]
    o_ref[...] = y


def kernel(x_prompt, x_sample, cache_k, cache_v, state_conv, page_table, norm1_g, w_in, conv_dw_w,
           conv_dw_b, conv_ln_g, conv_ln_b, w_conv_out, q_norm_g, k_norm_g, lambda_q1, lambda_k1,
           lambda_q2, lambda_k2, subln_g, w_attn_out, w_out, norm2_g, router_w, router_b,
           w_gate_up, b_gate_up, w_down, b_down):
    nb, seq, _ = x_prompt.shape
    db, t_new, _ = x_sample.shape
    n_p, n_s = nb * seq, db * t_new
    n_all = n_p + n_s
    assert n_p % ROW_TILE == 0 and n_all % DISPATCH_TILE == 0 and n_p % n_s == 0
    assert w_in.shape[0] == 1

    row = lambda a: a[0].reshape(1, -1).astype(F32)
    w_in_b = w_in[0].astype(BF16)
    wc_b = w_conv_out[0].astype(BF16)
    wa_b = w_attn_out[0].astype(BF16)
    wo_b = w_out[0].astype(BF16)
    rw = jnp.pad(router_w[0].astype(F32), ((0, 0), (0, LANES - N_EXPERTS)))
    rb = jnp.pad(router_b[0].astype(F32), (0, LANES - N_EXPERTS)).reshape(1, LANES)
    gq = jnp.tile(q_norm_g[0], QKV_W // DIFF_HD).reshape(1, QKV_W)
    gk = jnp.tile(k_norm_g[0], QKV_W // DIFF_HD).reshape(1, QKV_W)
    blk = jnp.arange(MXU_DIM) // DIFF_HD
    gmat = (blk[:, None] == blk[None, :]).astype(BF16)
    lam_p = jnp.stack([lambda_q1[0], lambda_k1[0], lambda_q2[0], lambda_k2[0]]).astype(F32)
    dw_w = jnp.pad(conv_dw_w[0], ((0, CONV_HALO - CONV_W), (0, 0)))

    xp = x_prompt.reshape(n_p, D_MODEL)
    xs_ = x_sample.reshape(n_s, D_MODEL)
    h_p = _rms_cast(xp, row(norm1_g))
    u_p = _glu_proj(h_p, w_in_b)
    q_p, k_p, kb_p, v_p, vb_p = _qkv_proj(h_p, w_in_b, gq, gk, gmat)
    sig_p = _gate_proj(h_p, w_in_b)
    z_s = _sample_inproj(xs_, row(norm1_g), w_in[0])
    u_s, q_s, k_s, v_s, sig_s = _sample_post(z_s, gq, gk, gmat)

    conv_args = (dw_w, row(conv_dw_b), row(conv_ln_g), row(conv_ln_b))
    c_p = _conv_prompt(u_p, nb, seq, *conv_args)
    ext_s = jnp.concatenate([state_conv[0], u_s.reshape(db, t_new, C_CONV)], axis=1)
    c_s = _conv_sample(ext_s, *conv_args).reshape(n_s, C_CONV)
    conv_prompt_new = u_p.reshape(nb, seq, C_CONV)[:, seq - (CONV_W - 1):][None]
    conv_sample_new = ext_s[:, t_new:][None]

    ot = _attn_prompt(_to_feature_major(q_p, ATTN_TILE), kb_p, _to_feature_major(vb_p, ATTN_KTILE),
                      nb, seq, lam_p, subln_g[0].reshape(HEAD_W, 1).astype(F32))
    o_p = jnp.transpose(ot, (0, 2, 1)).reshape(n_p, QKV_W)
    q_s = jnp.transpose(q_s.reshape(db, t_new, N_HEADS, HEAD_W), (0, 2, 1, 3))
    q_s = q_s.reshape(db, 1, N_HEADS * t_new, HEAD_W)
    in_map = ((jnp.arange(HEAD_W) // DIFF_HD)[None, :] == jnp.arange(2)[:, None])[None, :, None, :]
    qrows = jnp.where(in_map, q_s, jnp.zeros_like(q_s)).reshape(db, 2 * N_HEADS * t_new, HEAD_W)
    pad_new = ((0, 0), (0, LANES - t_new * N_HEADS), (0, 0))
    k_new = jnp.pad(k_s.astype(BF16).reshape(db, t_new * N_HEADS, HEAD_W), pad_new)
    v_new = jnp.pad(v_s.astype(BF16).reshape(db, t_new * N_HEADS, HEAD_W), pad_new)
    o_s = _attn_sample(qrows, cache_k[0], cache_v[0], page_table, k_new, v_new, lam_p,
                       row(subln_g), t_new)
    o_s = jnp.transpose(o_s.reshape(db, N_HEADS, t_new, HEAD_W), (0, 2, 1, 3)).reshape(n_s, QKV_W)

    route_w = (row(norm2_g), rw, rb)
    x1_all, idx_p, gw_p = _merge(xp, c_p, o_p, sig_p, wc_b, wa_b, wo_b, *route_w, n_all)
    merged_s = _sample_mix(c_s, o_s, sig_s, w_conv_out[0], w_attn_out[0])
    x1_all, idx_s, gw_s = _sample_out(xs_, merged_s, w_out[0], *route_w, x1_all)
    top_idx = jnp.concatenate([idx_p, idx_s], axis=0)
    gate_w = jnp.concatenate([gw_p, gw_s], axis=0)

    pos, sb_e, sb_start, sb_rows, n_rows = _routing_tables(top_idx)
    xs_sorted = _dispatch(x1_all, row(norm2_g), pos.reshape(-1), n_rows)
    ys = _experts(xs_sorted, sb_e, sb_start, sb_rows, w_gate_up[0], b_gate_up[0], w_down[0],
                  b_down[0])
    pos_flat = pos.reshape(-1)
    y_p = _combine_call(pos_flat, gate_w, x1_all, ys, n_p, COMBINE_TILE, 0)
    y_s = _combine_call(pos_flat, gate_w, x1_all, ys, n_s, n_s, n_p // n_s)

    kv_shape = lambda n, lead: (1, lead, n // lead, N_HEADS, HEAD_W)
    return (y_p.reshape(nb, seq, D_MODEL), y_s.reshape(db, t_new, D_MODEL),
            k_p.reshape(kv_shape(n_p, nb)), v_p.reshape(kv_shape(n_p, nb)), conv_prompt_new,
            k_s.reshape(kv_shape(n_s, db)), v_s.reshape(kv_shape(n_s, db)), conv_sample_new)


def _combine_call(pos_flat, gate_w, x1_all, ys, n, tile, block0):
    row0 = block0 * tile
    steps = n // tile
    pos3 = pos_flat[row0 * TOP_K:(row0 + n) * TOP_K].reshape(steps, 1, tile * TOP_K)
    gw = gate_w[row0:row0 + n]
    return pl.pallas_call(
        _combine_body,
        out_shape=jax.ShapeDtypeStruct((n, D_MODEL), F32),
        grid=(steps,),
        in_specs=[pl.BlockSpec((1, 1, tile * TOP_K), lambda i: (i, 0, 0),
                               memory_space=pltpu.SMEM),
                  pl.BlockSpec((tile, TOP_K), lambda i: (i, 0)),
                  pl.BlockSpec((tile, D_MODEL), lambda i: (block0 + i, 0)),
                  pl.BlockSpec(memory_space=pl.ANY)],
        out_specs=pl.BlockSpec((tile, D_MODEL), lambda i: (i, 0)),
        scratch_shapes=[pltpu.VMEM((TOP_K, tile, D_MODEL), F32), pltpu.SemaphoreType.DMA((1,))],
        compiler_params=_cparams(("arbitrary",), 40),
        name="moe_combine",
    )(pos3, gw, x1_all, ys)
```

```python
import functools
import math

import jax
import jax.numpy as jnp
from jax import lax
from jax.experimental import pallas as pl
from jax.experimental.pallas import tpu as pltpu

F32, BF16, I32 = jnp.float32, jnp.bfloat16, jnp.int32

D_MODEL = 2048
C_CONV = 1024
CONV_W = 31
N_HEADS = 8
DIFF_HD = 64
HEAD_W = 2 * DIFF_HD
QKV_W = N_HEADS * HEAD_W
N_EXPERTS = 32
TOP_K = 4
D_FF = 2048
SWIGLU_LIMIT = 7.0
SWIGLU_ALPHA = 1.702
RMS_EPS = 1e-6
LN_EPS = 1e-5
LAM_INIT = 0.8 - 0.6 * math.exp(-0.3 * 0)
PAGE = 128
MASK_VALUE = -1e30

LANES = 128
MXU_DIM = 256

ROW_TILE = 512
CONV_TILE = 256
CONV_CHUNK = 64
CONV_LANES = 256
CONV_LN_ROWS = 32
CONV_HALO = 32
ATTN_TILE = 512
ATTN_KTILE = 512
ATTN_HEADS = 2
DEC_PAGES = 16
MERGE_TILE = 512
MERGE_HALF = 256
MOE_SUB = 256
MOE_TAIL = 128
MOE_CAP = 2304
MOE_FF = 256
DISPATCH_TILE = 384
COMBINE_TILE = 256


def _cparams(semantics, vmem_mb=None):
    kw = dict(dimension_semantics=semantics)
    if vmem_mb is not None:
        kw["vmem_limit_bytes"] = vmem_mb * 1024 * 1024
    return pltpu.CompilerParams(**kw)


def _sigmoid(x):
    return 1.0 / (1.0 + jnp.exp(-x))


def _dot(a, b):
    return jnp.dot(a, b, preferred_element_type=F32)


def _rms(x, g):
    ms = jnp.mean(x * x, axis=-1, keepdims=True)
    return x * lax.rsqrt(ms + RMS_EPS) * g


def _hi_lo(v):
    hi = v.astype(BF16)
    return hi, (v - hi.astype(F32)).astype(BF16)


def _dot3(a, w):
    a_hi, a_lo = _hi_lo(a)
    w_hi, w_lo = _hi_lo(w)
    return _dot(a_hi, w_hi) + _dot(a_lo, w_hi) + _dot(a_hi, w_lo)


def _rms_body(x_ref, g_ref, o_ref):
    o_ref[...] = _rms(x_ref[...], g_ref[...]).astype(o_ref.dtype)


def _rms_cast(x, g):
    n = x.shape[0]
    return pl.pallas_call(
        _rms_body,
        out_shape=jax.ShapeDtypeStruct((n, D_MODEL), BF16),
        grid=(n // ROW_TILE,),
        in_specs=[pl.BlockSpec((ROW_TILE, D_MODEL), lambda i: (i, 0)),
                  pl.BlockSpec((1, D_MODEL), lambda i: (0, 0))],
        out_specs=pl.BlockSpec((ROW_TILE, D_MODEL), lambda i: (i, 0)),
        compiler_params=_cparams(("arbitrary",)),
        name="rms_in",
    )(x, g)


IN_COLS = 2 * C_CONV + 3 * QKV_W + 2 * D_MODEL


def _sample_inproj_body(x_ref, g_ref, w_ref, z_ref):
    z_ref[...] = _dot3(_rms(x_ref[...], g_ref[...]), w_ref[...])


def _sample_inproj(x, g, w):
    n = x.shape[0]
    return pl.pallas_call(
        _sample_inproj_body,
        out_shape=jax.ShapeDtypeStruct((n, IN_COLS), F32),
        grid=(IN_COLS // SAMPLE_COLS,),
        in_specs=[pl.BlockSpec((n, D_MODEL), lambda j: (0, 0)),
                  pl.BlockSpec((1, D_MODEL), lambda j: (0, 0)),
                  pl.BlockSpec((D_MODEL, SAMPLE_COLS), lambda j: (0, j))],
        out_specs=pl.BlockSpec((n, SAMPLE_COLS), lambda j: (0, j)),
        compiler_params=_cparams(("arbitrary",), 40),
        name="sample_inproj",
    )(x, g, w)


def _sample_post_body(z_ref, gq_ref, gk_ref, gm_ref, u_ref, q_ref, k_ref, v_ref, sig_ref):
    o1 = 2 * C_CONV
    o2, o3, o4 = o1 + QKV_W, o1 + 2 * QKV_W, o1 + 3 * QKV_W
    gmat = gm_ref[...]
    u_ref[...] = z_ref[:, 0:C_CONV] * _sigmoid(z_ref[:, C_CONV:o1])
    q = _group_rms(z_ref[:, o1:o2], gq_ref[...], gmat)
    q_ref[...] = (q * (DIFF_HD ** -0.5)).astype(BF16)
    k_ref[...] = _group_rms(z_ref[:, o2:o3], gk_ref[...], gmat)
    v_ref[...] = z_ref[:, o3:o4]
    sig_ref[...] = _sigmoid(z_ref[:, o4:])


def _sample_post(z, gq, gk, gmat):
    n = z.shape[0]
    full = lambda a: pl.BlockSpec(a.shape, lambda: (0,) * a.ndim)
    sds = lambda w, dt: jax.ShapeDtypeStruct((n, w), dt)
    outs = (sds(C_CONV, F32), sds(QKV_W, BF16), sds(QKV_W, F32), sds(QKV_W, F32),
            sds(2 * D_MODEL, F32))
    return pl.pallas_call(
        _sample_post_body,
        out_shape=outs,
        in_specs=[full(z), full(gq), full(gk), full(gmat)],
        out_specs=tuple(pl.BlockSpec(o.shape, lambda: (0, 0)) for o in outs),
        name="sample_post",
    )(z, gq, gk, gmat)


def _glu_body(h_ref, wa_ref, wg_ref, u_ref):
    h = h_ref[...]
    a = _dot(h, wa_ref[...])
    g = _dot(h, wg_ref[...])
    u_ref[...] = a * _sigmoid(g)


def _glu_proj(h, w_in):
    n = h.shape[0]
    return pl.pallas_call(
        _glu_body,
        out_shape=jax.ShapeDtypeStruct((n, C_CONV), F32),
        grid=(n // ROW_TILE,),
        in_specs=[pl.BlockSpec((ROW_TILE, D_MODEL), lambda i: (i, 0)),
                  pl.BlockSpec((D_MODEL, C_CONV), lambda i: (0, 0)),
                  pl.BlockSpec((D_MODEL, C_CONV), lambda i: (0, 1))],
        out_specs=pl.BlockSpec((ROW_TILE, C_CONV), lambda i: (i, 0)),
        compiler_params=_cparams(("arbitrary",), 48),
        name="glu_proj",
    )(h, w_in, w_in)


def _group_rms(z, g, gmat):
    zz = z * z
    hi = zz.astype(BF16)
    lo = (zz - hi.astype(F32)).astype(BF16)
    cols = []
    for c in range(QKV_W // MXU_DIM):
        sl = slice(c * MXU_DIM, (c + 1) * MXU_DIM)
        cols.append(_dot(hi[:, sl], gmat) + _dot(lo[:, sl], gmat))
    ss = jnp.concatenate(cols, axis=1)
    return z * lax.rsqrt(ss * (1.0 / DIFF_HD) + RMS_EPS) * g


def _qkv_body(h_ref, wq_ref, wk_ref, wv_ref, gq_ref, gk_ref, gm_ref,
              q_ref, kf_ref, kb_ref, vf_ref, vb_ref):
    h = h_ref[...]
    gmat = gm_ref[...]
    q = _group_rms(_dot(h, wq_ref[...]), gq_ref[...], gmat)
    q_ref[...] = (q * (DIFF_HD ** -0.5)).astype(BF16)
    k = _group_rms(_dot(h, wk_ref[...]), gk_ref[...], gmat)
    kf_ref[...] = k
    kb_ref[...] = k.astype(BF16)
    v = _dot(h, wv_ref[...])
    vf_ref[...] = v
    vb_ref[...] = v.astype(BF16)


def _qkv_proj(h, w_in, gq, gk, gmat):
    n = h.shape[0]
    wspec = lambda j: pl.BlockSpec((D_MODEL, QKV_W), lambda i: (0, j))
    row = lambda w: pl.BlockSpec((ROW_TILE, w), lambda i: (i, 0))
    const = lambda r, c: pl.BlockSpec((r, c), lambda i: (0, 0))
    sds = lambda dt: jax.ShapeDtypeStruct((n, QKV_W), dt)
    return pl.pallas_call(
        _qkv_body,
        out_shape=(sds(BF16), sds(F32), sds(BF16), sds(F32), sds(BF16)),
        grid=(n // ROW_TILE,),
        in_specs=[row(D_MODEL), wspec(2), wspec(3), wspec(4),
                  const(1, QKV_W), const(1, QKV_W), const(MXU_DIM, MXU_DIM)],
        out_specs=(row(QKV_W),) * 5,
        compiler_params=_cparams(("arbitrary",), 56),
        name="qkv_proj",
    )(h, w_in, w_in, w_in, gq, gk, gmat)


def _gate_body(h_ref, w_ref, o_ref):
    o_ref[...] = _sigmoid(_dot(h_ref[...], w_ref[...]))


def _gate_proj(h, w_in):
    n = h.shape[0]
    first = (2 * C_CONV + 3 * QKV_W) // QKV_W
    return pl.pallas_call(
        _gate_body,
        out_shape=jax.ShapeDtypeStruct((n, 2 * D_MODEL), F32),
        grid=(2 * D_MODEL // QKV_W, n // ROW_TILE),
        in_specs=[pl.BlockSpec((ROW_TILE, D_MODEL), lambda j, i: (i, 0)),
                  pl.BlockSpec((D_MODEL, QKV_W), lambda j, i: (0, first + j))],
        out_specs=pl.BlockSpec((ROW_TILE, QKV_W), lambda j, i: (i, j)),
        compiler_params=_cparams(("arbitrary", "arbitrary"), 40),
        name="gate_proj",
    )(h, w_in)


def _ln_swish(c, g, b):
    mu = jnp.mean(c, axis=-1, keepdims=True)
    d = c - mu
    var = jnp.mean(d * d, axis=-1, keepdims=True)
    y = d * lax.rsqrt(var + LN_EPS) * g + b
    return y * _sigmoid(y)


def _conv_body(u_ref, w_ref, b_ref, g_ref, beta_ref, o_ref, ext_ref, c_ref):
    i = pl.program_id(1)

    @pl.when(i == 0)
    def _():
        ext_ref[0:CONV_HALO, :] = jnp.zeros((CONV_HALO, C_CONV), F32)

    @pl.when(i > 0)
    def _():
        ext_ref[0:CONV_HALO, :] = ext_ref[CONV_TILE:CONV_TILE + CONV_HALO, :]

    ext_ref[CONV_HALO:, :] = u_ref[...]
    lead = CONV_HALO - (CONV_W - 1)
    for r0 in range(0, CONV_TILE, CONV_CHUNK):
        for c0 in range(0, C_CONV, CONV_LANES):
            cols = slice(c0, c0 + CONV_LANES)
            out = None
            for b in range(8):
                rows = CONV_CHUNK + (8 if b else 0)
                part = None
                for a in range((lead + CONV_W + 7) // 8):
                    jp = 8 * a + b
                    if jp < lead or jp >= lead + CONV_W:
                        continue
                    term = ext_ref[r0 + 8 * a:r0 + 8 * a + rows, cols] * w_ref[jp:jp + 1, cols]
                    part = term if part is None else part + term
                if part is None:
                    continue
                shifted = part[b:b + CONV_CHUNK, :]
                out = shifted if out is None else out + shifted
            c_ref[r0:r0 + CONV_CHUNK, cols] = out + b_ref[:, cols]
    for r0 in range(0, CONV_TILE, CONV_LN_ROWS):
        y = _ln_swish(c_ref[r0:r0 + CONV_LN_ROWS, :], g_ref[...], beta_ref[...])
        o_ref[r0:r0 + CONV_LN_ROWS, :] = y.astype(o_ref.dtype)


def _conv_prompt(u_all, n_batch, seq, dw_w, dw_b, ln_g, ln_b):
    nt = seq // CONV_TILE
    const = lambda r: pl.BlockSpec((r, C_CONV), lambda b, i: (0, 0))
    lead = CONV_HALO - (CONV_W - 1)
    taps = jnp.pad(dw_w[:CONV_W], ((lead, -(lead + CONV_W) % 8), (0, 0)))
    return pl.pallas_call(
        _conv_body,
        out_shape=jax.ShapeDtypeStruct((n_batch * seq, C_CONV), BF16),
        grid=(n_batch, nt),
        in_specs=[pl.BlockSpec((CONV_TILE, C_CONV), lambda b, i: (b * nt + i, 0)),
                  const(taps.shape[0]), const(1), const(1), const(1)],
        out_specs=pl.BlockSpec((CONV_TILE, C_CONV), lambda b, i: (b * nt + i, 0)),
        scratch_shapes=[pltpu.VMEM((CONV_HALO + CONV_TILE, C_CONV), F32),
                        pltpu.VMEM((CONV_TILE, C_CONV), F32)],
        compiler_params=_cparams(("arbitrary", "arbitrary")),
        name="conv_prompt",
    )(u_all, taps, dw_b, ln_g, ln_b)


def _conv_sample_body(ext_ref, w_ref, b_ref, g_ref, beta_ref, o_ref, *, t_new):
    acc = jnp.zeros((t_new, C_CONV), F32) + b_ref[...]
    for j in range(CONV_W):
        acc = acc + ext_ref[0, j:j + t_new, :] * w_ref[j:j + 1, :]
    y = _ln_swish(acc, g_ref[...], beta_ref[...])
    o_ref[0] = y.astype(o_ref.dtype)


def _conv_sample(ext, dw_w, dw_b, ln_g, ln_b):
    nb, rows, _ = ext.shape
    t_new = rows - (CONV_W - 1)
    const = lambda r: pl.BlockSpec((r, C_CONV), lambda b: (0, 0))
    return pl.pallas_call(
        functools.partial(_conv_sample_body, t_new=t_new),
        out_shape=jax.ShapeDtypeStruct((nb, t_new, C_CONV), F32),
        grid=(nb,),
        in_specs=[pl.BlockSpec((1, rows, C_CONV), lambda b: (b, 0, 0)),
                  const(CONV_HALO), const(1), const(1), const(1)],
        out_specs=pl.BlockSpec((1, t_new, C_CONV), lambda b: (b, 0, 0)),
        compiler_params=_cparams(("arbitrary",)),
        name="conv_sample",
    )(ext, dw_w, dw_b, ln_g, ln_b)


def _lambda_value(lam_ref):
    lp = lam_ref[...]
    a = jnp.sum(lp[0:1, :] * lp[1:2, :], axis=-1, keepdims=True)
    b = jnp.sum(lp[2:3, :] * lp[3:4, :], axis=-1, keepdims=True)
    return jnp.exp(a) - jnp.exp(b) + LAM_INIT


def _subln(o, g):
    return _rms(o, g) * (1.0 - LAM_INIT)


def _attn_body(qt_ref, k_ref, vt_ref, lam_ref, g_ref, ot_ref, m_ref, l_ref, acc_ref):
    nq, _, t = qt_ref.shape
    tk = vt_ref.shape[2]
    per_q = t // tk
    lam = _lambda_value(lam_ref)
    lane = lax.broadcasted_iota(I32, (tk, HEAD_W), 1)
    key_row = lax.broadcasted_iota(I32, (tk, t), 0)
    query_col = lax.broadcasted_iota(I32, (tk, t), 1)
    heads = [slice(h * HEAD_W, (h + 1) * HEAD_W) for h in range(ATTN_HEADS)]

    for qi in range(nq):
        m_ref[...] = jnp.full(m_ref.shape, MASK_VALUE, F32)
        l_ref[...] = jnp.zeros(l_ref.shape, F32)
        acc_ref[...] = jnp.zeros(acc_ref.shape, F32)

        def tile(ki, key_offset, qi=qi):
            rows = pl.ds(pl.multiple_of(ki * tk, tk), tk)
            for h, hs in enumerate(heads):
                k = k_ref[rows, hs]
                vt = vt_ref[ki, hs, :]
                qt = qt_ref[qi, hs, :]
                zero = jnp.zeros_like(k)
                for m in range(2):
                    c = 2 * h + m
                    in_map = (lane < DIFF_HD) if m == 0 else (lane >= DIFF_HD)
                    st = _dot(jnp.where(in_map, k, zero), qt)
                    if key_offset is not None:
                        st = jnp.where(key_row + key_offset <= query_col, st, MASK_VALUE)
                    m_prev = m_ref[c]
                    m_new = jnp.maximum(m_prev, jnp.max(st, axis=0, keepdims=True))
                    alpha = jnp.exp(m_prev - m_new)
                    p = jnp.exp(st - m_new)
                    l_ref[c] = alpha * l_ref[c] + jnp.sum(p, axis=0, keepdims=True)
                    acc_ref[c] = alpha * acc_ref[c] + _dot(vt, p.astype(BF16))
                    m_ref[c] = m_new

        if qi > 0:
            def off_diagonal(ki, carry):
                tile(ki, None)
                return carry
            lax.fori_loop(0, qi * per_q, off_diagonal, 0)
        for d in range(per_q):
            tile(qi * per_q + d, d * tk)

        for h, hs in enumerate(heads):
            o = (acc_ref[2 * h] / l_ref[2 * h]
                 - lam * (acc_ref[2 * h + 1] / l_ref[2 * h + 1]))
            ms = jnp.mean(o * o, axis=0, keepdims=True)
            o = o * lax.rsqrt(ms + RMS_EPS) * g_ref[...] * (1.0 - LAM_INIT)
            ot_ref[qi, hs, :] = o.astype(ot_ref.dtype)


def _attn_prompt(qt, k_all, vt, n_batch, seq, lam_p, subln_col):
    nq = seq // ATTN_TILE
    nk = seq // ATTN_KTILE
    width = ATTN_HEADS * HEAD_W
    qspec = pl.BlockSpec((nq, width, ATTN_TILE), lambda b, h: (b, h, 0))
    return pl.pallas_call(
        _attn_body,
        out_shape=jax.ShapeDtypeStruct(qt.shape, BF16),
        grid=(n_batch, N_HEADS // ATTN_HEADS),
        in_specs=[qspec, pl.BlockSpec((seq, width), lambda b, h: (b, h)),
                  pl.BlockSpec((nk, width, ATTN_KTILE), lambda b, h: (b, h, 0)),
                  pl.BlockSpec((4, DIFF_HD), lambda b, h: (0, 0)),
                  pl.BlockSpec((HEAD_W, 1), lambda b, h: (0, 0))],
        out_specs=qspec,
        scratch_shapes=[pltpu.VMEM((2 * ATTN_HEADS, 1, ATTN_TILE), F32),
                        pltpu.VMEM((2 * ATTN_HEADS, 1, ATTN_TILE), F32),
                        pltpu.VMEM((2 * ATTN_HEADS, HEAD_W, ATTN_TILE), F32)],
        compiler_params=_cparams(("arbitrary", "arbitrary")),
        name="attn_prompt",
    )(qt, k_all, vt, lam_p, subln_col)


def _to_feature_major(x, tile):
    return jnp.transpose(x.reshape(x.shape[0] // tile, tile, x.shape[1]), (0, 2, 1))


def _dec_body(pt_ref, q_ref, *refs, t_new):
    del pt_ref
    k_refs = refs[:DEC_PAGES]
    v_refs = refs[DEC_PAGES:2 * DEC_PAGES]
    knew_ref, vnew_ref, lam_ref, g_ref, o_ref, m_ref, l_ref, acc_ref = refs[2 * DEC_PAGES:]
    p = pl.program_id(1)
    page_rows = PAGE * N_HEADS
    half = N_HEADS * t_new

    @pl.when(p == 0)
    def _():
        m_ref[...] = jnp.full(m_ref.shape, MASK_VALUE, F32)
        l_ref[...] = jnp.zeros(l_ref.shape, F32)
        acc_ref[...] = jnp.zeros(acc_ref.shape, F32)

    qrows = q_ref[0]

    def head_mask(n_keys):
        r = lax.broadcasted_iota(I32, (2 * half, n_keys), 0)
        c = lax.broadcasted_iota(I32, (2 * half, n_keys), 1)
        return r, c, (c % N_HEADS) == ((r % half) // t_new)

    def update(kbs, vbs, keep):
        scores = [jnp.where(keep, lax.dot_general(qrows, kb, (((1,), (1,)), ((), ())),
                                                  preferred_element_type=F32), MASK_VALUE)
                  for kb in kbs]
        m_prev = m_ref[...]
        m_new = m_prev
        for s in scores:
            m_new = jnp.maximum(m_new, jnp.max(s, axis=1, keepdims=True))
        alpha = jnp.exp(m_prev - m_new)
        l_new = alpha * l_ref[...]
        acc = alpha * acc_ref[...]
        for s, vb in zip(scores, vbs):
            pr = jnp.exp(s - m_new)
            l_new = l_new + jnp.sum(pr, axis=1, keepdims=True)
            acc = acc + _dot(pr.astype(BF16), vb)
        l_ref[...] = l_new
        acc_ref[...] = acc
        m_ref[...] = m_new

    _, _, keep_page = head_mask(page_rows)
    update([k_refs[j][0].astype(BF16) for j in range(DEC_PAGES)],
           [v_refs[j][0].astype(BF16) for j in range(DEC_PAGES)], keep_page)

    @pl.when(p == pl.num_programs(1) - 1)
    def _():
        r, c, keep = head_mask(knew_ref.shape[1])
        causal = ((c // N_HEADS) <= (r % t_new)) & (c < half)
        update([knew_ref[0]], [vnew_ref[0]], keep & causal)
        o_maps = acc_ref[...] / l_ref[...]
        lam = _lambda_value(lam_ref)
        o = o_maps[0:half, :] - lam * o_maps[half:2 * half, :]
        o_ref[0] = _subln(o, g_ref[...]).astype(o_ref.dtype)


def _attn_sample(qcols, cache_k, cache_v, page_table, k_new, v_new, lam_p, subln_g, t_new):
    nb = qcols.shape[0]
    n_pages = page_table.shape[1]
    steps = n_pages // DEC_PAGES
    page_rows = PAGE * N_HEADS
    ck = cache_k.reshape(cache_k.shape[0], page_rows, HEAD_W)
    cv = cache_v.reshape(cache_v.shape[0], page_rows, HEAD_W)
    pt = page_table.reshape(-1)

    def page_spec(j):
        return pl.BlockSpec((1, page_rows, HEAD_W),
                            lambda b, p, pt_ref: (pt_ref[b * n_pages + p * DEC_PAGES + j], 0, 0))

    per_b = lambda r, c: pl.BlockSpec((1, r, c), lambda b, p, pt_ref: (b, 0, 0))
    const = lambda r, c: pl.BlockSpec((r, c), lambda b, p, pt_ref: (0, 0))
    rows_new = t_new * N_HEADS
    q_rows = 2 * rows_new
    grid_spec = pltpu.PrefetchScalarGridSpec(
        num_scalar_prefetch=1,
        grid=(nb, steps),
        in_specs=[per_b(q_rows, HEAD_W)]
                 + [page_spec(j) for j in range(DEC_PAGES)] * 2
                 + [per_b(k_new.shape[1], HEAD_W), per_b(v_new.shape[1], HEAD_W),
                    const(4, DIFF_HD), const(1, HEAD_W)],
        out_specs=per_b(rows_new, HEAD_W),
        scratch_shapes=[pltpu.VMEM((q_rows, 1), F32), pltpu.VMEM((q_rows, 1), F32),
                        pltpu.VMEM((q_rows, HEAD_W), F32)],
    )
    return pl.pallas_call(
        functools.partial(_dec_body, t_new=t_new),
        out_shape=jax.ShapeDtypeStruct((nb, rows_new, HEAD_W), F32),
        grid_spec=grid_spec,
        compiler_params=_cparams(("arbitrary", "arbitrary")),
        name="attn_sample",
    )(pt, qcols, *([ck] * DEC_PAGES), *([cv] * DEC_PAGES), k_new, v_new, lam_p, subln_g)


def _route(x1, g2_ref, rw_ref, rb_ref, idx_ref, gw_ref):
    tm = x1.shape[0]
    logits = _dot3(_rms(x1, g2_ref[...]), rw_ref[...]) + rb_ref[...]
    lane = lax.broadcasted_iota(I32, (tm, LANES), 1)
    v = jnp.where(lane < N_EXPERTS, logits, -jnp.inf)
    kcol = lax.broadcasted_iota(I32, (tm, TOP_K), 1)
    idx_out = jnp.zeros((tm, TOP_K), I32)
    val_out = jnp.zeros((tm, TOP_K), F32)
    for k in range(TOP_K):
        mx = jnp.max(v, axis=1, keepdims=True)
        ix = jnp.min(jnp.where(v == mx, lane, LANES), axis=1, keepdims=True)
        idx_out = jnp.where(kcol == k, ix, idx_out)
        val_out = jnp.where(kcol == k, mx, val_out)
        v = jnp.where(lane == ix, -jnp.inf, v)
    e = jnp.exp(val_out - val_out[:, 0:1])
    idx_ref[...] = idx_out
    gw_ref[...] = e / jnp.sum(e, axis=1, keepdims=True)


def _merge_body(x_ref, c_ref, o_ref, sc_ref, sa_ref, wc_ref, wa_ref, wo_ref, g2_ref,
                rw_ref, rb_ref, x1_ref, idx_ref, gw_ref):
    for r0 in range(0, MERGE_TILE, MERGE_HALF):
        rows = slice(r0, r0 + MERGE_HALF)
        c_out = _dot(c_ref[rows, :], wc_ref[...])
        a_out = _dot(o_ref[rows, :], wa_ref[...])
        merged = sc_ref[rows, :] * c_out + sa_ref[rows, :] * a_out
        x1 = x_ref[rows, :] + _dot(merged.astype(BF16), wo_ref[...])
        x1_ref[rows, :] = x1
        _route(x1, g2_ref, rw_ref, rb_ref, idx_ref.at[rows], gw_ref.at[rows])


def _merge(x, c_act, o_act, sig, wc, wa, wo, g2, rw, rb, total_rows):
    n = x.shape[0]
    tile = MERGE_TILE
    row = lambda w: pl.BlockSpec((tile, w), lambda i: (i, 0))
    const = lambda r, c: pl.BlockSpec((r, c), lambda i: (0, 0), pipeline_mode=pl.Buffered(1))
    return pl.pallas_call(
        _merge_body,
        out_shape=(jax.ShapeDtypeStruct((total_rows, D_MODEL), F32),
                   jax.ShapeDtypeStruct((n, TOP_K), I32),
                   jax.ShapeDtypeStruct((n, TOP_K), F32)),
        grid=(n // tile,),
        in_specs=[row(D_MODEL), row(C_CONV), row(QKV_W),
                  pl.BlockSpec((tile, D_MODEL), lambda i: (i, 0)),
                  pl.BlockSpec((tile, D_MODEL), lambda i: (i, 1)),
                  const(C_CONV, D_MODEL), const(QKV_W, D_MODEL), const(D_MODEL, D_MODEL),
                  const(1, D_MODEL), const(D_MODEL, LANES), const(1, LANES)],
        out_specs=(row(D_MODEL), row(TOP_K), row(TOP_K)),
        compiler_params=_cparams(("arbitrary",), 60),
        name="merge_router",
    )(x, c_act, o_act, sig, sig, wc, wa, wo, g2, rw, rb)


SAMPLE_COLS = 512


def _sample_mix_body(c_ref, o_ref, sc_ref, sa_ref, wc_ref, wa_ref, m_ref):
    m_ref[...] = (sc_ref[...] * _dot3(c_ref[...], wc_ref[...])
                  + sa_ref[...] * _dot3(o_ref[...], wa_ref[...]))


def _sample_mix(c_act, o_act, sig, wc, wa):
    n = c_act.shape[0]
    nblk = D_MODEL // SAMPLE_COLS
    full = lambda w: pl.BlockSpec((n, w), lambda j: (0, 0))
    wspec = pl.BlockSpec((C_CONV, SAMPLE_COLS), lambda j: (0, j))
    return pl.pallas_call(
        _sample_mix_body,
        out_shape=jax.ShapeDtypeStruct((n, D_MODEL), F32),
        grid=(nblk,),
        in_specs=[full(C_CONV), full(QKV_W),
                  pl.BlockSpec((n, SAMPLE_COLS), lambda j: (0, j)),
                  pl.BlockSpec((n, SAMPLE_COLS), lambda j: (0, nblk + j)),
                  wspec, wspec],
        out_specs=pl.BlockSpec((n, SAMPLE_COLS), lambda j: (0, j)),
        compiler_params=_cparams(("arbitrary",)),
        name="sample_mix",
    )(c_act, o_act, sig, sig, wc, wa)


def _sample_out_body(x_ref, m_ref, wo_ref, g2_ref, rw_ref, rb_ref, prev_ref,
                     x1_ref, idx_ref, gw_ref, acc_ref):
    del prev_ref
    kb = pl.program_id(0)

    @pl.when(kb == 0)
    def _():
        acc_ref[...] = x_ref[...]

    acc_ref[...] = acc_ref[...] + _dot3(m_ref[...], wo_ref[...])

    @pl.when(kb == pl.num_programs(0) - 1)
    def _():
        x1 = acc_ref[...]
        x1_ref[...] = x1
        _route(x1, g2_ref, rw_ref, rb_ref, idx_ref, gw_ref)


def _sample_out(x, merged, wo, g2, rw, rb, x1_prompt):
    n = x.shape[0]
    total_rows = x1_prompt.shape[0]
    assert (total_rows - n) % n == 0
    const = lambda r, c: pl.BlockSpec((r, c), lambda k: (0, 0))
    wspec = pl.BlockSpec((SAMPLE_COLS, D_MODEL), lambda k: (k, 0))
    return pl.pallas_call(
        _sample_out_body,
        out_shape=(jax.ShapeDtypeStruct((total_rows, D_MODEL), F32),
                   jax.ShapeDtypeStruct((n, TOP_K), I32),
                   jax.ShapeDtypeStruct((n, TOP_K), F32)),
        grid=(D_MODEL // SAMPLE_COLS,),
        in_specs=[const(n, D_MODEL), pl.BlockSpec((n, SAMPLE_COLS), lambda k: (0, k)),
                  wspec, const(1, D_MODEL), const(D_MODEL, LANES), const(1, LANES),
                  pl.BlockSpec(memory_space=pl.ANY)],
        out_specs=(pl.BlockSpec((n, D_MODEL), lambda k: ((total_rows - n) // n, 0)),
                   const(n, TOP_K), const(n, TOP_K)),
        scratch_shapes=[pltpu.VMEM((n, D_MODEL), F32)],
        input_output_aliases={6: 0},
        compiler_params=_cparams(("arbitrary",)),
        name="sample_out",
    )(x, merged, wo, g2, rw, rb, x1_prompt)


def _rank_body(idx_ref, pos_ref, cnt_ref, gs_ref, carry_ref):
    phase = pl.program_id(0)
    i = pl.program_id(1)
    tb = idx_ref.shape[0]
    lane = lax.broadcasted_iota(I32, (tb, LANES), 1)
    idx = idx_ref[...]
    hits = [lane == idx[:, k:k + 1] for k in range(TOP_K)]
    onehot = jnp.zeros((tb, LANES), F32)
    for hit in hits:
        onehot = onehot + hit.astype(F32)

    @pl.when((phase == 0) & (i == 0))
    def _():
        cnt_ref[...] = jnp.zeros(cnt_ref.shape, F32)

    @pl.when(phase == 0)
    def _():
        cnt_ref[...] = cnt_ref[...] + jnp.sum(onehot, axis=0, keepdims=True)

    @pl.when((phase == 1) & (i == 0))
    def _():
        blocks = jnp.floor((cnt_ref[...] + (MOE_SUB - 1)) * (1.0 / MOE_SUB))
        r = lax.broadcasted_iota(I32, (LANES, LANES), 0)
        c = lax.broadcasted_iota(I32, (LANES, LANES), 1)
        before = (r < c).astype(BF16)
        blocks8 = jnp.broadcast_to(blocks, (8, LANES)).astype(BF16)
        gs_ref[...] = _dot(blocks8, before)[0:1, :] * float(MOE_SUB)
        carry_ref[...] = jnp.zeros(carry_ref.shape, F32)

    @pl.when(phase == 1)
    def _():
        r = lax.broadcasted_iota(I32, (tb, tb), 0)
        c = lax.broadcasted_iota(I32, (tb, tb), 1)
        earlier = (c < r).astype(BF16)
        base = gs_ref[...] + carry_ref[...] + _dot(earlier, onehot.astype(BF16))
        kcol = lax.broadcasted_iota(I32, (tb, TOP_K), 1)
        pos = jnp.zeros((tb, TOP_K), F32)
        for k, hit in enumerate(hits):
            pk = jnp.sum(jnp.where(hit, base, 0.0), axis=1, keepdims=True)
            pos = jnp.where(kcol == k, pk, pos)
        pos_ref[...] = pos.astype(I32)
        carry_ref[...] = carry_ref[...] + jnp.sum(onehot, axis=0, keepdims=True)


def _rank(top_idx):
    n = top_idx.shape[0]
    tb = DISPATCH_TILE
    return pl.pallas_call(
        _rank_body,
        out_shape=(jax.ShapeDtypeStruct((n, TOP_K), I32),
                   jax.ShapeDtypeStruct((1, LANES), F32),
                   jax.ShapeDtypeStruct((1, LANES), F32)),
        grid=(2, n // tb),
        in_specs=[pl.BlockSpec((tb, TOP_K), lambda p, i: (i, 0))],
        out_specs=(pl.BlockSpec((tb, TOP_K), lambda p, i: (i * p, 0)),
                   pl.BlockSpec((1, LANES), lambda p, i: (0, 0)),
                   pl.BlockSpec((1, LANES), lambda p, i: (0, 0))),
        scratch_shapes=[pltpu.VMEM((1, LANES), F32)],
        compiler_params=_cparams(("arbitrary", "arbitrary")),
        name="moe_rank",
    )(top_idx)


def _routing_tables(top_idx):
    n = top_idx.shape[0]
    nk = n * TOP_K
    pos, counts_f, gstart_f = _rank(top_idx)
    counts = counts_f[0, :N_EXPERTS].astype(I32)
    gstart = gstart_f[0, :N_EXPERTS].astype(I32)

    n_sb_max = nk // MOE_CAP + N_EXPERTS
    nsb = (counts + MOE_CAP - 1) // MOE_CAP
    cum = jnp.cumsum(nsb)
    total = cum[-1]
    s = jnp.arange(n_sb_max, dtype=I32)
    ex = jnp.minimum(jnp.sum((cum[None, :] <= s[:, None]).astype(I32), axis=1), N_EXPERTS - 1)
    ex_last = ex[jnp.maximum(total - 1, 0)]
    valid = s < total
    ex = jnp.where(valid, ex, ex_last)
    local = s - (cum[ex] - nsb[ex])
    rows = jnp.where(valid, jnp.clip(counts[ex] - local * MOE_CAP, 0, MOE_CAP), 0).astype(I32)
    start = jnp.where(valid, gstart[ex] + local * MOE_CAP, 0).astype(I32)
    n_rows = (nk + N_EXPERTS * (MOE_SUB - 1) + MOE_SUB - 1) // MOE_SUB * MOE_SUB
    return pos, ex, start, rows, n_rows


def _dispatch_body(pos_ref, x_ref, g_ref, xs_ref, hbuf, sem):
    i = pl.program_id(0)
    n = pl.num_programs(0)
    tb = x_ref.shape[0]
    slot = lax.rem(i, 2)

    def wait_slot(sl):
        pltpu.make_async_copy(xs_ref.at[pl.ds(0, TOP_K * tb)], xs_ref.at[pl.ds(0, TOP_K * tb)],
                              sem.at[sl]).wait()

    @pl.when(i >= 2)
    def _():
        wait_slot(slot)

    hbuf[slot] = _rms(x_ref[...], g_ref[...])

    def body(t, carry):
        for k in range(TOP_K):
            p = pos_ref[0, 0, t * TOP_K + k]
            pltpu.make_async_copy(hbuf.at[slot, pl.ds(t, 1)], xs_ref.at[pl.ds(p, 1)],
                                  sem.at[slot]).start(priority=k % 2)
        return carry

    lax.fori_loop(0, tb, body, 0, unroll=4)

    @pl.when(i == n - 1)
    def _():
        wait_slot(slot)
        wait_slot(1 - slot)


def _dispatch(x1_all, g2, pos, n_rows):
    n = x1_all.shape[0]
    tb = DISPATCH_TILE
    steps = n // tb
    assert steps >= 2
    pos3 = pos.reshape(steps, 1, tb * TOP_K)
    return pl.pallas_call(
        _dispatch_body,
        out_shape=jax.ShapeDtypeStruct((n_rows, D_MODEL), F32),
        grid=(steps,),
        in_specs=[pl.BlockSpec((1, 1, tb * TOP_K), lambda i: (i, 0, 0), memory_space=pltpu.SMEM),
                  pl.BlockSpec((tb, D_MODEL), lambda i: (i, 0)),
                  pl.BlockSpec((1, D_MODEL), lambda i: (0, 0))],
        out_specs=pl.BlockSpec(memory_space=pl.ANY),
        scratch_shapes=[pltpu.VMEM((2, tb, D_MODEL), F32), pltpu.SemaphoreType.DMA((2,))],
        compiler_params=_cparams(("arbitrary",)),
        name="moe_dispatch",
    )(pos3, x1_all, g2)


def _expert_body(sbe_ref, sbs_ref, sbr_ref, xs_ref, wg_ref, wu_ref, wd_ref, bg_ref, bu_ref,
                 bd_ref, ys_ref, x_buf, y_buf, wgu_bf, wd_bf, a_buf, x_stage, sem_in, sem_out):
    del sbe_ref
    s = pl.program_id(0)
    f = pl.program_id(1)
    n_f = pl.num_programs(1)
    rows = sbr_ref[s]
    start = pl.multiple_of(sbs_ref[s], MOE_SUB)
    nsub = lax.shift_right_logical(rows + (MOE_SUB - 1), int(math.log2(MOE_SUB)))

    def sub_rows(j):
        return pl.ds(pl.multiple_of(j * MOE_SUB, MOE_SUB), MOE_SUB)

    def copy_in(j, sl):
        return pltpu.make_async_copy(xs_ref.at[pl.ds(start + j * MOE_SUB, MOE_SUB)],
                                     x_stage.at[sl], sem_in.at[sl])

    def copy_out(j):
        return pltpu.make_async_copy(y_buf.at[sub_rows(j)],
                                     ys_ref.at[pl.ds(start + j * MOE_SUB, MOE_SUB)],
                                     sem_out.at[0])

    @pl.when(rows > 0)
    def _():
        wgu_bf[:, 0:MOE_FF] = wg_ref[0].astype(BF16)
        wgu_bf[:, MOE_FF:] = wu_ref[0].astype(BF16)
        wd_bf[...] = wd_ref[0].astype(BF16)

        def fetch(j):
            sl = lax.rem(j, 2)
            copy_in(j, sl).wait()
            rid = lax.broadcasted_iota(I32, (MOE_SUB, D_MODEL), 0)
            xv = jnp.where(rid < rows - j * MOE_SUB, x_stage[sl], 0.0)
            x_buf[sub_rows(j), :] = xv.astype(BF16)

            @pl.when(j + 2 < nsub)
            def _():
                copy_in(j + 2, sl).start()

            y_buf[sub_rows(j), :] = jnp.broadcast_to(bd_ref[0], (MOE_SUB, D_MODEL))

        def activate(j, slot, nrows=MOE_SUB):
            r = pl.ds(pl.multiple_of(j * MOE_SUB, MOE_SUB), nrows)
            gu = _dot(x_buf[r, :], wgu_bf[...])
            g = jnp.minimum(gu[:, 0:MOE_FF] + bg_ref[0], SWIGLU_LIMIT)
            u = jnp.clip(gu[:, MOE_FF:] + bu_ref[0], -SWIGLU_LIMIT, SWIGLU_LIMIT)
            a = (u + 1.0) * (g * _sigmoid(SWIGLU_ALPHA * g))
            a_buf[slot, 0:nrows, :] = a.astype(BF16)

        def project(j, slot, nrows=MOE_SUB):
            r = pl.ds(pl.multiple_of(j * MOE_SUB, MOE_SUB), nrows)
            y_buf[r, :] = y_buf[r, :] + _dot(a_buf[slot, 0:nrows, :], wd_bf[...])

        rem = rows - (nsub - 1) * MOE_SUB
        has_tail = rem <= MOE_TAIL
        nmain = nsub - has_tail.astype(I32)

        def run(first, last):
            if first:
                copy_in(0, 0).start()

                @pl.when(nsub > 1)
                def _():
                    copy_in(1, 1).start()

            def step(j, slot):
                if first:
                    fetch(j)
                activate(j, slot)
                project(j - 1, 1 - slot)
                if last:
                    copy_out(j - 1).start()

            @pl.when(nmain > 0)
            def _():
                if first:
                    fetch(0)
                activate(0, 0)

                def pair(i, carry):
                    j = 2 * i + 1
                    step(j, 1)
                    step(j + 1, 0)
                    return carry

                lax.fori_loop(0, lax.shift_right_logical(nmain - 1, 1), pair, 0)

                @pl.when(lax.rem(nmain - 1, 2) == 1)
                def _():
                    step(nmain - 1, 1)

                project(nmain - 1, lax.rem(nmain - 1, 2))
                if last:
                    copy_out(nmain - 1).start()

            @pl.when(has_tail)
            def _():
                if first:
                    fetch(nmain)
                activate(nmain, 0, MOE_TAIL)
                project(nmain, 0, MOE_TAIL)
                if last:
                    copy_out(nmain).start()

            if last:
                def drain(j, carry):
                    copy_out(j).wait()
                    return carry

                lax.fori_loop(0, nsub, drain, 0)

        n_f_static = D_FF // MOE_FF

        @pl.when(f == 0)
        def _():
            run(True, n_f_static == 1)

        if n_f_static > 2:
            @pl.when((f > 0) & (f < n_f - 1))
            def _():
                run(False, False)

        if n_f_static > 1:
            @pl.when(f == n_f - 1)
            def _():
                run(False, True)


def _experts(xs, sb_e, sb_start, sb_rows, w_gate_up, b_gate_up, w_down, b_down):
    n_rows = xs.shape[0]
    n_sb = sb_e.shape[0]
    n_f = D_FF // MOE_FF

    def fsel(s, f, sbr):
        return jnp.where(sbr[s] > 0, f, n_f - 1)

    wg = pl.BlockSpec((1, D_MODEL, MOE_FF), lambda s, f, e, st, r: (e[s], 0, fsel(s, f, r)))
    wu = pl.BlockSpec((1, D_MODEL, MOE_FF), lambda s, f, e, st, r: (e[s], 0, n_f + fsel(s, f, r)))
    wd = pl.BlockSpec((1, MOE_FF, D_MODEL), lambda s, f, e, st, r: (e[s], fsel(s, f, r), 0))
    bg = pl.BlockSpec((1, 1, MOE_FF), lambda s, f, e, st, r: (e[s], 0, fsel(s, f, r)))
    bu = pl.BlockSpec((1, 1, MOE_FF), lambda s, f, e, st, r: (e[s], 0, n_f + fsel(s, f, r)))
    bd = pl.BlockSpec((1, 1, D_MODEL), lambda s, f, e, st, r: (e[s], 0, 0))
    grid_spec = pltpu.PrefetchScalarGridSpec(
        num_scalar_prefetch=3,
        grid=(n_sb, n_f),
        in_specs=[pl.BlockSpec(memory_space=pl.ANY), wg, wu, wd, bg, bu, bd],
        out_specs=pl.BlockSpec(memory_space=pl.ANY),
        scratch_shapes=[pltpu.VMEM((MOE_CAP, D_MODEL), BF16),
                        pltpu.VMEM((MOE_CAP, D_MODEL), F32),
                        pltpu.VMEM((D_MODEL, 2 * MOE_FF), BF16),
                        pltpu.VMEM((MOE_FF, D_MODEL), BF16),
                        pltpu.VMEM((2, MOE_SUB, MOE_FF), BF16),
                        pltpu.VMEM((2, MOE_SUB, D_MODEL), F32),
                        pltpu.SemaphoreType.DMA((2,)),
                        pltpu.SemaphoreType.DMA((1,))],
    )
    return pl.pallas_call(
        _expert_body,
        out_shape=jax.ShapeDtypeStruct((n_rows, D_MODEL), F32),
        grid_spec=grid_spec,
        compiler_params=_cparams(("arbitrary", "arbitrary"), 60),
        name="moe_experts",
    )(sb_e, sb_start, sb_rows, xs, w_gate_up, w_gate_up, w_down,
      b_gate_up.reshape(N_EXPERTS, 1, 2 * D_FF), b_gate_up.reshape(N_EXPERTS, 1, 2 * D_FF),
      b_down.reshape(N_EXPERTS, 1, D_MODEL))


def _combine_body(pos_ref, gw_ref, x1_ref, ys_ref, o_ref, ybuf, sem):
    tb = x1_ref.shape[0]

    def body(t, carry):
        for k in range(TOP_K):
            p = pos_ref[0, 0, t * TOP_K + k]
            pltpu.make_async_copy(ys_ref.at[pl.ds(p, 1)], ybuf.at[k, pl.ds(t, 1)],
                                  sem.at[0]).start(priority=k % 2)
        return carry

    lax.fori_loop(0, tb, body, 0, unroll=4)
    pltpu.make_async_copy(ybuf, ybuf, sem.at[0]).wait()
    gw = gw_ref[...]
    y = x1_ref[...]
    for k in range(TOP_K):
        y = y + gw[:, k:k + 1] * ybuf[k]
    o_ref[...] = y


def kernel(x_prompt, x_sample, cache_k, cache_v, state_conv, page_table, norm1_g, w_in, conv_dw_w,
           conv_dw_b, conv_ln_g, conv_ln_b, w_conv_out, q_norm_g, k_norm_g, lambda_q1, lambda_k1,
           lambda_q2, lambda_k2, subln_g, w_attn_out, w_out, norm2_g, router_w, router_b,
           w_gate_up, b_gate_up, w_down, b_down):
    nb, seq, _ = x_prompt.shape
    db, t_new, _ = x_sample.shape
    n_p, n_s = nb * seq, db * t_new
    n_all = n_p + n_s
    assert n_p % ROW_TILE == 0 and n_all % DISPATCH_TILE == 0 and n_p % n_s == 0
    assert w_in.shape[0] == 1

    row = lambda a: a[0].reshape(1, -1).astype(F32)
    w_in_b = w_in[0].astype(BF16)
    wc_b = w_conv_out[0].astype(BF16)
    wa_b = w_attn_out[0].astype(BF16)
    wo_b = w_out[0].astype(BF16)
    rw = jnp.pad(router_w[0].astype(F32), ((0, 0), (0, LANES - N_EXPERTS)))
    rb = jnp.pad(router_b[0].astype(F32), (0, LANES - N_EXPERTS)).reshape(1, LANES)
    gq = jnp.tile(q_norm_g[0], QKV_W // DIFF_HD).reshape(1, QKV_W)
    gk = jnp.tile(k_norm_g[0], QKV_W // DIFF_HD).reshape(1, QKV_W)
    blk = jnp.arange(MXU_DIM) // DIFF_HD
    gmat = (blk[:, None] == blk[None, :]).astype(BF16)
    lam_p = jnp.stack([lambda_q1[0], lambda_k1[0], lambda_q2[0], lambda_k2[0]]).astype(F32)
    dw_w = jnp.pad(conv_dw_w[0], ((0, CONV_HALO - CONV_W), (0, 0)))

    xp = x_prompt.reshape(n_p, D_MODEL)
    xs_ = x_sample.reshape(n_s, D_MODEL)
    h_p = _rms_cast(xp, row(norm1_g))
    u_p = _glu_proj(h_p, w_in_b)
    q_p, k_p, kb_p, v_p, vb_p = _qkv_proj(h_p, w_in_b, gq, gk, gmat)
    sig_p = _gate_proj(h_p, w_in_b)
    z_s = _sample_inproj(xs_, row(norm1_g), w_in[0])
    u_s, q_s, k_s, v_s, sig_s = _sample_post(z_s, gq, gk, gmat)

    conv_args = (dw_w, row(conv_dw_b), row(conv_ln_g), row(conv_ln_b))
    c_p = _conv_prompt(u_p, nb, seq, *conv_args)
    ext_s = jnp.concatenate([state_conv[0], u_s.reshape(db, t_new, C_CONV)], axis=1)
    c_s = _conv_sample(ext_s, *conv_args).reshape(n_s, C_CONV)
    conv_prompt_new = u_p.reshape(nb, seq, C_CONV)[:, seq - (CONV_W - 1):][None]
    conv_sample_new = ext_s[:, t_new:][None]

    ot = _attn_prompt(_to_feature_major(q_p, ATTN_TILE), kb_p, _to_feature_major(vb_p, ATTN_KTILE),
                      nb, seq, lam_p, subln_g[0].reshape(HEAD_W, 1).astype(F32))
    o_p = jnp.transpose(ot, (0, 2, 1)).reshape(n_p, QKV_W)
    q_s = jnp.transpose(q_s.reshape(db, t_new, N_HEADS, HEAD_W), (0, 2, 1, 3))
    q_s = q_s.reshape(db, 1, N_HEADS * t_new, HEAD_W)
    in_map = ((jnp.arange(HEAD_W) // DIFF_HD)[None, :] == jnp.arange(2)[:, None])[None, :, None, :]
    qrows = jnp.where(in_map, q_s, jnp.zeros_like(q_s)).reshape(db, 2 * N_HEADS * t_new, HEAD_W)
    pad_new = ((0, 0), (0, LANES - t_new * N_HEADS), (0, 0))
    k_new = jnp.pad(k_s.astype(BF16).reshape(db, t_new * N_HEADS, HEAD_W), pad_new)
    v_new = jnp.pad(v_s.astype(BF16).reshape(db, t_new * N_HEADS, HEAD_W), pad_new)
    o_s = _attn_sample(qrows, cache_k[0], cache_v[0], page_table, k_new, v_new, lam_p,
                       row(subln_g), t_new)
    o_s = jnp.transpose(o_s.reshape(db, N_HEADS, t_new, HEAD_W), (0, 2, 1, 3)).reshape(n_s, QKV_W)

    route_w = (row(norm2_g), rw, rb)
    x1_all, idx_p, gw_p = _merge(xp, c_p, o_p, sig_p, wc_b, wa_b, wo_b, *route_w, n_all)
    merged_s = _sample_mix(c_s, o_s, sig_s, w_conv_out[0], w_attn_out[0])
    x1_all, idx_s, gw_s = _sample_out(xs_, merged_s, w_out[0], *route_w, x1_all)
    top_idx = jnp.concatenate([idx_p, idx_s], axis=0)
    gate_w = jnp.concatenate([gw_p, gw_s], axis=0)

    pos, sb_e, sb_start, sb_rows, n_rows = _routing_tables(top_idx)
    xs_sorted = _dispatch(x1_all, row(norm2_g), pos.reshape(-1), n_rows)
    ys = _experts(xs_sorted, sb_e, sb_start, sb_rows, w_gate_up[0], b_gate_up[0], w_down[0],
                  b_down[0])
    pos_flat = pos.reshape(-1)
    y_p = _combine_call(pos_flat, gate_w, x1_all, ys, n_p, COMBINE_TILE, 0)
    y_s = _combine_call(pos_flat, gate_w, x1_all, ys, n_s, n_s, n_p // n_s)

    kv_shape = lambda n, lead: (1, lead, n // lead, N_HEADS, HEAD_W)
    return (y_p.reshape(nb, seq, D_MODEL), y_s.reshape(db, t_new, D_MODEL),
            k_p.reshape(kv_shape(n_p, nb)), v_p.reshape(kv_shape(n_p, nb)), conv_prompt_new,
            k_s.reshape(kv_shape(n_s, db)), v_s.reshape(kv_shape(n_s, db)), conv_sample_new)


def _combine_call(pos_flat, gate_w, x1_all, ys, n, tile, block0):
    row0 = block0 * tile
    steps = n // tile
    pos3 = pos_flat[row0 * TOP_K:(row0 + n) * TOP_K].reshape(steps, 1, tile * TOP_K)
    gw = gate_w[row0:row0 + n]
    return pl.pallas_call(
        _combine_body,
        out_shape=jax.ShapeDtypeStruct((n, D_MODEL), F32),
        grid=(steps,),
        in_specs=[pl.BlockSpec((1, 1, tile * TOP_K), lambda i: (i, 0, 0),
                               memory_space=pltpu.SMEM),
                  pl.BlockSpec((tile, TOP_K), lambda i: (i, 0)),
                  pl.BlockSpec((tile, D_MODEL), lambda i: (block0 + i, 0)),
                  pl.BlockSpec(memory_space=pl.ANY)],
        out_specs=pl.BlockSpec((tile, D_MODEL), lambda i: (i, 0)),
        scratch_shapes=[pltpu.VMEM((TOP_K, tile, D_MODEL), F32), pltpu.SemaphoreType.DMA((1,))],
        compiler_params=_cparams(("arbitrary",), 40),
        name="moe_combine",
    )(pos3, gw, x1_all, ys)
```
